```python
import jax, jax.numpy as jnp
from jax import lax
import numpy as np

D_MODEL = 1024
BATCH = 8
SEQ = 2048
DEPTH = 4
DEC_BATCH = 8
DEC_SEQ = 64
PAST_LEN = 2048

CHUNK = 64
Q_BLOCK = 128
HEAD_DIM = 64
N_HEADS = (3 * D_MODEL // 4) // HEAD_DIM
MIX_WIDTH = N_HEADS * HEAD_DIM
MEM_WIDTH = D_MODEL // 4
N_MEM_HEADS = MEM_WIDTH // HEAD_DIM
N_MEM = 256
D_FF = -(-8 * D_MODEL // (3 * 256)) * 256
N_FOX = (DEPTH + 1) // 2
N_SB = DEPTH // 2
FOX_IN = 3 * MIX_WIDTH + N_HEADS + MEM_WIDTH
SB_IN = 3 * MIX_WIDTH + MEM_WIDTH
EPS = 1e-6
SCALE = HEAD_DIM ** -0.5

kernel_name = "fox_stickbreak_memory_streaming_step"


def rmsnorm(x, g):
    xf = x.astype(jnp.float32)
    y = xf * lax.rsqrt(jnp.mean(xf * xf, axis=-1, keepdims=True) + EPS)
    return (y * g.astype(jnp.float32)).astype(x.dtype)


def to_blocks(a):
    b, t = a.shape[:2]
    return jnp.moveaxis(a.reshape((b, t // Q_BLOCK, Q_BLOCK) + a.shape[2:]), 1, 0)


def from_blocks(a):
    a = jnp.moveaxis(a, 0, 1)
    return a.reshape((a.shape[0], a.shape[1] * a.shape[2]) + a.shape[3:])


def fox_block(q, k, v, cq, ck, qpos, kpos):
    s = jnp.einsum('bqhd,bkhd->bhqk', q.astype(jnp.float32), k.astype(jnp.float32)) * SCALE
    s = s + jnp.transpose(cq, (0, 2, 1))[..., :, None] - jnp.transpose(ck, (0, 2, 1))[..., None, :]
    mask = kpos[None, :] <= qpos[:, None]
    p = jax.nn.softmax(jnp.where(mask, s, -jnp.inf), axis=-1)
    return jnp.einsum('bhqk,bkhd->bqhd', p.astype(v.dtype), v)


def sb_block(q, k, v, qpos, kpos):
    z = jnp.einsum('bqhd,bkhd->bhqk', q.astype(jnp.float32), k.astype(jnp.float32)) * SCALE
    mask = kpos[None, :] < qpos[:, None]
    u = jnp.where(mask, jax.nn.log_sigmoid(-z), 0.0)
    rest = lax.cumsum(u, axis=3, reverse=True) - u
    a = jnp.where(mask, jnp.exp(jax.nn.log_sigmoid(z) + rest), 0.0)
    return jnp.einsum('bhqk,bkhd->bqhd', a.astype(v.dtype), v)


def mem_attend(qm, mk, mv):
    s = jnp.einsum('bqhd,bmhd->bhqm', qm.astype(jnp.float32), mk.astype(jnp.float32)) * SCALE
    p = jax.nn.softmax(s, axis=-1)
    return jnp.einsum('bhqm,bmhd->bqhd', p.astype(mv.dtype), mv)


def heads(a, n):
    b, t = a.shape[:2]
    return a.reshape(b, t, n, HEAD_DIM)


def fox_mixer(hn, w_in, b_f, past):
    proj = hn @ w_in
    q, k, v, fg, qm = jnp.split(proj, [MIX_WIDTH, 2 * MIX_WIDTH, 3 * MIX_WIDTH, 3 * MIX_WIDTH + N_HEADS], axis=-1)
    q, k, v = heads(q, N_HEADS), heads(k, N_HEADS), heads(v, N_HEADS)
    logf = jax.nn.log_sigmoid((fg + b_f).astype(jnp.float32))
    t = hn.shape[1]
    if past is None:
        c = jnp.cumsum(logf, axis=1)
        pos = jnp.arange(t)
        def body(a):
            qb, cb, pb = a
            return fox_block(qb, k, v, cb, c, pb, pos)
        o = from_blocks(lax.map(body, (to_blocks(q), to_blocks(c), pos.reshape(-1, Q_BLOCK))))
    else:
        pk, pv, plf = past
        p_len = pk.shape[1]
        k_all = jnp.concatenate([pk.astype(k.dtype), k], axis=1)
        v_all = jnp.concatenate([pv.astype(v.dtype), v], axis=1)
        c_all = jnp.cumsum(jnp.concatenate([plf.astype(jnp.float32), logf], axis=1), axis=1)
        o = fox_block(q, k_all, v_all, c_all[:, p_len:], c_all, p_len + jnp.arange(t), jnp.arange(p_len + t))
    return o, heads(qm, N_MEM_HEADS), (k, v, logf)


def sb_mixer(hn, w_in, past):
    proj = hn @ w_in
    q, k, v, qm = jnp.split(proj, [MIX_WIDTH, 2 * MIX_WIDTH, 3 * MIX_WIDTH], axis=-1)
    q, k, v = heads(q, N_HEADS), heads(k, N_HEADS), heads(v, N_HEADS)
    t = hn.shape[1]
    if past is None:
        pos = jnp.arange(t)
        def body(a):
            qb, pb = a
            return sb_block(qb, k, v, pb, pos)
        o = from_blocks(lax.map(body, (to_blocks(q), pos.reshape(-1, Q_BLOCK))))
    else:
        pk, pv = past
        p_len = pk.shape[1]
        k_all = jnp.concatenate([pk.astype(k.dtype), k], axis=1)
        v_all = jnp.concatenate([pv.astype(v.dtype), v], axis=1)
        o = sb_block(q, k_all, v_all, p_len + jnp.arange(t), jnp.arange(p_len + t))
    return o, heads(qm, N_MEM_HEADS), (k, v)


def finish_layer(x, o_mix, qm, mk, mv, w_out, g_ffn, w_gate_up, w_down):
    b, t = x.shape[:2]
    o_mem = mem_attend(qm, mk, mv)
    o = jnp.concatenate([o_mix.reshape(b, t, MIX_WIDTH), o_mem.reshape(b, t, MEM_WIDTH)], axis=-1)
    x = x + o @ w_out
    gate, up = jnp.split(rmsnorm(x, g_ffn) @ w_gate_up, 2, axis=-1)
    return x + (jax.nn.silu(gate) * up) @ w_down


def setup_inputs(seed: int = 0) -> dict:
    key = jax.random.key(seed)
    ks = jax.random.split(key, 24)
    nrm = lambda k, shape, s=1.0: jax.random.normal(k, shape, jnp.float32) * s
    return {
        "x_prompt": nrm(ks[0], (BATCH, SEQ, D_MODEL)),
        "x_sample": nrm(ks[1], (DEC_BATCH, DEC_SEQ, D_MODEL)),
        "mem_prompt": nrm(ks[2], (BATCH, N_MEM, D_MODEL)),
        "cache_fox_k": nrm(ks[3], (N_FOX, DEC_BATCH, PAST_LEN, N_HEADS, HEAD_DIM)),
        "cache_fox_v": nrm(ks[4], (N_FOX, DEC_BATCH, PAST_LEN, N_HEADS, HEAD_DIM)),
        "cache_fox_logf": jax.nn.log_sigmoid(2.0 + nrm(ks[5], (N_FOX, DEC_BATCH, PAST_LEN, N_HEADS))),
        "cache_sb_k": nrm(ks[6], (N_SB, DEC_BATCH, PAST_LEN, N_HEADS, HEAD_DIM)),
        "cache_sb_v": nrm(ks[7], (N_SB, DEC_BATCH, PAST_LEN, N_HEADS, HEAD_DIM)),
        "cache_mem_k": nrm(ks[8], (DEPTH, DEC_BATCH, N_MEM, N_MEM_HEADS, HEAD_DIM)),
        "cache_mem_v": nrm(ks[9], (DEPTH, DEC_BATCH, N_MEM, N_MEM_HEADS, HEAD_DIM)),
        "g_mix": 1.0 + nrm(ks[10], (DEPTH, D_MODEL), 0.02),
        "w_in_fox": nrm(ks[11], (N_FOX, D_MODEL, FOX_IN), D_MODEL ** -0.5),
        "b_f": 2.0 + nrm(ks[12], (N_FOX, N_HEADS), 0.5),
        "w_in_sb": nrm(ks[13], (N_SB, D_MODEL, SB_IN), D_MODEL ** -0.5),
        "g_mem": 1.0 + nrm(ks[14], (DEPTH, D_MODEL), 0.02),
        "w_mem_kv": nrm(ks[15], (DEPTH, D_MODEL, 2 * MEM_WIDTH), D_MODEL ** -0.5),
        "w_out": nrm(ks[16], (DEPTH, MIX_WIDTH + MEM_WIDTH, D_MODEL), (MIX_WIDTH + MEM_WIDTH) ** -0.5),
        "g_ffn": 1.0 + nrm(ks[17], (DEPTH, D_MODEL), 0.02),
        "w_gate_up": nrm(ks[18], (DEPTH, D_MODEL, 2 * D_FF), D_MODEL ** -0.5),
        "w_down": nrm(ks[19], (DEPTH, D_FF, D_MODEL), D_FF ** -0.5),
        "g_final": 1.0 + nrm(ks[20], (D_MODEL,), 0.02),
    }


def reference(x_prompt, x_sample, mem_prompt, cache_fox_k, cache_fox_v, cache_fox_logf, cache_sb_k, cache_sb_v,
              cache_mem_k, cache_mem_v, g_mix, w_in_fox, b_f, w_in_sb, g_mem, w_mem_kv, w_out, g_ffn,
              w_gate_up, w_down, g_final):
    xp, xs = x_prompt, x_sample
    fk_p, fv_p, fl_p, sk_p, sv_p, mk_p_l, mv_p_l = [], [], [], [], [], [], []
    fk_s, fv_s, fl_s, sk_s, sv_s = [], [], [], [], []
    for i in range(DEPTH):
        j = i // 2
        mkv = rmsnorm(mem_prompt, g_mem[i]) @ w_mem_kv[i]
        mk_p, mv_p = jnp.split(mkv, 2, axis=-1)
        mk_p, mv_p = heads(mk_p, N_MEM_HEADS), heads(mv_p, N_MEM_HEADS)
        mk_p_l.append(mk_p)
        mv_p_l.append(mv_p)
        hp = rmsnorm(xp, g_mix[i])
        hs = rmsnorm(xs, g_mix[i])
        if i % 2 == 0:
            op, qmp, (k, v, lf) = fox_mixer(hp, w_in_fox[j], b_f[j], None)
            os_, qms, (k2, v2, lf2) = fox_mixer(hs, w_in_fox[j], b_f[j],
                                                (cache_fox_k[j], cache_fox_v[j], cache_fox_logf[j]))
            fk_p.append(k); fv_p.append(v); fl_p.append(lf)
            fk_s.append(k2); fv_s.append(v2); fl_s.append(lf2)
        else:
            op, qmp, (k, v) = sb_mixer(hp, w_in_sb[j], None)
            os_, qms, (k2, v2) = sb_mixer(hs, w_in_sb[j], (cache_sb_k[j], cache_sb_v[j]))
            sk_p.append(k); sv_p.append(v)
            sk_s.append(k2); sv_s.append(v2)
        xp = finish_layer(xp, op, qmp, mk_p, mv_p, w_out[i], g_ffn[i], w_gate_up[i], w_down[i])
        xs = finish_layer(xs, os_, qms, cache_mem_k[i].astype(xs.dtype), cache_mem_v[i].astype(xs.dtype),
                          w_out[i], g_ffn[i], w_gate_up[i], w_down[i])
    y_prompt = rmsnorm(xp, g_final)
    y_sample = rmsnorm(xs, g_final)
    return (y_prompt, y_sample,
            jnp.stack(fk_p), jnp.stack(fv_p), jnp.stack(fl_p), jnp.stack(sk_p), jnp.stack(sv_p),
            jnp.stack(mk_p_l), jnp.stack(mv_p_l),
            jnp.stack(fk_s), jnp.stack(fv_s), jnp.stack(fl_s), jnp.stack(sk_s), jnp.stack(sv_s))
```

```python
import functools

import jax
import jax.numpy as jnp
from jax import lax
from jax.experimental import pallas as pl
from jax.experimental.pallas import tpu as pltpu

HEAD_DIM = 64
LANES = 128
HEADS_PER_TILE = LANES // HEAD_DIM
EPS = 1e-6
SCALE = HEAD_DIM ** -0.5
NEG = -1e30
VMEM_LIMIT = 56 * 1024 * 1024

F32 = jnp.float32
BF16 = jnp.bfloat16
NT_DIMS = (((1,), (1,)), ((), ()))


def _rms_scale(x):
    return x * lax.rsqrt(jnp.mean(x * x, axis=-1, keepdims=True) + EPS)


def _softplus(z):
    return jnp.maximum(z, 0.0) + jnp.log(1.0 + jnp.exp(-jnp.abs(z)))


def _const_spec(shape):
    return pl.BlockSpec(shape, lambda *_: (0,) * len(shape), pipeline_mode=pl.Buffered(1))


def _proj_kernel(*refs, mix, memw, has_forget):
    if has_forget:
        (x_ref, g_ref, w_ref, bf_ref,
         q_ref, k_ref, v_ref, kb_ref, vb_ref, qm_ref, lf_ref) = refs
    else:
        (x_ref, g_ref, w_ref,
         q_ref, k_ref, v_ref, kb_ref, vb_ref, qm_ref) = refs
    hn = (_rms_scale(x_ref[...]) * g_ref[...]).astype(BF16)

    def seg(lo, width):
        return jnp.dot(hn, w_ref[:, lo:lo + width], preferred_element_type=F32)

    q_ref[...] = seg(0, mix).astype(BF16)
    k = seg(mix, mix)
    k_ref[...] = k
    kb_ref[...] = k.astype(BF16)
    v = seg(2 * mix, mix)
    v_ref[...] = v
    vb_ref[...] = v.astype(BF16)
    qm_ref[...] = seg(3 * mix, memw).astype(BF16)
    if has_forget:
        f = seg(3 * mix + memw, LANES) + bf_ref[...]
        lf_ref[...] = -_softplus(-f)


def _project(x, g, w, bias, *, mix, memw, tm):
    rows, d = x.shape
    has_forget = bias is not None
    row = lambda width: pl.BlockSpec((tm, width), lambda r: (r, 0))
    in_specs = [row(d), _const_spec((1, d)), _const_spec(w.shape)]
    args = [x, g.reshape(1, d), w]
    out_shape = [jax.ShapeDtypeStruct((rows, mix), BF16),
                 jax.ShapeDtypeStruct((rows, mix), F32),
                 jax.ShapeDtypeStruct((rows, mix), F32),
                 jax.ShapeDtypeStruct((rows, mix), BF16),
                 jax.ShapeDtypeStruct((rows, mix), BF16),
                 jax.ShapeDtypeStruct((rows, memw), BF16)]
    out_specs = [row(mix)] * 5 + [row(memw)]
    if has_forget:
        in_specs.append(_const_spec((1, LANES)))
        args.append(bias)
        out_shape.append(jax.ShapeDtypeStruct((rows, LANES), F32))
        out_specs.append(row(LANES))
    return pl.pallas_call(
        functools.partial(_proj_kernel, mix=mix, memw=memw, has_forget=has_forget),
        grid=(rows // tm,),
        in_specs=in_specs, out_specs=out_specs, out_shape=out_shape,
        compiler_params=pltpu.CompilerParams(
            dimension_semantics=("arbitrary",), vmem_limit_bytes=VMEM_LIMIT),
        name="proj_fox" if has_forget else "proj_sb",
    )(*args)


def _memkv_kernel(x_ref, g_ref, w_ref, k_ref, v_ref, kb_ref, vb_ref, *, memw):
    hn = (_rms_scale(x_ref[...]) * g_ref[0]).astype(BF16)
    kv = jnp.dot(hn, w_ref[0], preferred_element_type=F32)
    k = kv[:, :memw]
    v = kv[:, memw:]
    k_ref[0] = k
    v_ref[0] = v
    kb_ref[0] = k.astype(BF16)
    vb_ref[0] = v.astype(BF16)


def _memory_kv(mem, g_mem, w_mem_kv, *, tm):
    rows, d = mem.shape
    depth, _, two_memw = w_mem_kv.shape
    memw = two_memw // 2
    out_spec = pl.BlockSpec((1, tm, memw), lambda l, r: (l, r, 0))
    return pl.pallas_call(
        functools.partial(_memkv_kernel, memw=memw),
        grid=(depth, rows // tm),
        in_specs=[pl.BlockSpec((tm, d), lambda l, r: (r, 0)),
                  pl.BlockSpec((1, 1, d), lambda l, r: (l, 0, 0)),
                  pl.BlockSpec((1, d, two_memw), lambda l, r: (l, 0, 0))],
        out_specs=[out_spec] * 4,
        out_shape=[jax.ShapeDtypeStruct((depth, rows, memw), F32)] * 2
        + [jax.ShapeDtypeStruct((depth, rows, memw), BF16)] * 2,
        compiler_params=pltpu.CompilerParams(
            dimension_semantics=("arbitrary", "arbitrary"), vmem_limit_bytes=VMEM_LIMIT),
        name="memkv",
    )(mem, g_mem.reshape(depth, 1, d), w_mem_kv)


def _split3(x):
    hi = x.astype(BF16)
    r1 = x - hi.astype(F32)
    mid = r1.astype(BF16)
    lo = (r1 - mid.astype(F32)).astype(BF16)
    return hi, mid, lo


def _cumsum_kernel(lf_ref, c_ref, *, t_len, rows_out):
    r = lax.broadcasted_iota(jnp.int32, (LANES, LANES), 0)
    c = lax.broadcasted_iota(jnp.int32, (LANES, LANES), 1)
    upper = (r <= c).astype(BF16)
    carry = jnp.zeros((rows_out, 1), F32)
    for ch in range(t_len // LANES):
        blk = lf_ref[0, ch * LANES:(ch + 1) * LANES, :].T[:rows_out, :]
        cs = carry
        for part in _split3(blk):
            cs = cs + jnp.dot(part, upper, preferred_element_type=F32)
        c_ref[0, :, ch * LANES:(ch + 1) * LANES] = cs
        carry = cs[:, LANES - 1:LANES]


def _cumulative_logf(lf_pad, rows_out=16):
    b, t_len, _ = lf_pad.shape
    return pl.pallas_call(
        functools.partial(_cumsum_kernel, t_len=t_len, rows_out=rows_out),
        grid=(b,),
        in_specs=[pl.BlockSpec((1, t_len, LANES), lambda i: (i, 0, 0))],
        out_specs=pl.BlockSpec((1, rows_out, t_len), lambda i: (i, 0, 0)),
        out_shape=jax.ShapeDtypeStruct((b, rows_out, t_len), F32),
        compiler_params=pltpu.CompilerParams(
            dimension_semantics=("arbitrary",), vmem_limit_bytes=VMEM_LIMIT),
        name="cumsum_logf",
    )(lf_pad)


def _attn_kernel(*refs, mode, tq, tk, off):
    if mode == "fox":
        q_ref, k_ref, v_ref, c_ref, o_ref = refs
    else:
        q_ref, k_ref, v_ref, o_ref = refs
    i = pl.program_id(2)
    q = q_ref[0]
    q_start = off + i * tq
    jd = q_start // tk
    lane = lax.broadcasted_iota(jnp.int32, (tq, LANES), 1)
    row_pos = q_start + lax.broadcasted_iota(jnp.int32, (tq, tk), 0)
    col = lax.broadcasted_iota(jnp.int32, (tq, tk), 1)

    if mode == "sb":
        r = lax.broadcasted_iota(jnp.int32, (tk, tk), 0)
        c = lax.broadcasted_iota(jnp.int32, (tk, tk), 1)
        later = (r > c).astype(BF16)

    def kv_block(j):
        start = pl.multiple_of(j * tk, tk)
        return k_ref[0, pl.ds(start, tk), :], v_ref[0, pl.ds(start, tk), :]

    head_out = []
    for hh in range(HEADS_PER_TILE):
        in_head = (lane >= hh * HEAD_DIM) & (lane < (hh + 1) * HEAD_DIM)
        qh = jnp.where(in_head, q, jnp.zeros_like(q))

        if mode == "fox":
            def step(j, carry, masked, hh=hh, qh=qh):
                m, l, acc = carry
                kb, vb = kv_block(j)
                s = lax.dot_general(qh, kb, NT_DIMS, preferred_element_type=F32)
                s = s - c_ref[0, 0, j, hh:hh + 1, :]
                if masked:
                    s = jnp.where(j * tk + col <= row_pos, s, NEG)
                m_new = jnp.maximum(m, jnp.max(s, axis=1, keepdims=True))
                alpha = jnp.exp(m - m_new)
                p = jnp.exp(s - m_new)
                l = alpha * l + jnp.sum(p, axis=1, keepdims=True)
                acc = alpha * acc + jnp.dot(p.astype(BF16), vb, preferred_element_type=F32)
                return m_new, l, acc

            init = (jnp.full((tq, 1), NEG, F32), jnp.zeros((tq, 1), F32),
                    jnp.zeros((tq, LANES), F32))
            carry = lax.fori_loop(0, jd, functools.partial(step, masked=False), init)
            _, l, acc = step(jd, carry, True)
            head_out.append(acc / l)
        else:
            def step(j, carry, masked, qh=qh):
                run, acc = carry
                kb, vb = kv_block(j)
                z = lax.dot_general(qh, kb, NT_DIMS, preferred_element_type=F32)
                u = -_softplus(z)
                if masked:
                    visible = j * tk + col < row_pos
                    u = jnp.where(visible, u, 0.0)
                hi = u.astype(BF16)
                lo = (u - hi.astype(F32)).astype(BF16)
                rest = (jnp.dot(hi, later, preferred_element_type=F32)
                        + jnp.dot(lo, later, preferred_element_type=F32))
                a = jnp.exp(z + u + rest + run)
                if masked:
                    a = jnp.where(visible, a, 0.0)
                acc = acc + jnp.dot(a.astype(BF16), vb, preferred_element_type=F32)
                return run + jnp.sum(u, axis=1, keepdims=True), acc

            init = (jnp.zeros((tq, 1), F32), jnp.zeros((tq, LANES), F32))
            carry = step(jd, init, True)
            _, acc = lax.fori_loop(
                0, jd, lambda t, cy: step(jd - 1 - t, cy, False), carry)
            head_out.append(acc)

    o_ref[0] = jnp.where(lane < HEAD_DIM, head_out[0], head_out[1]).astype(o_ref.dtype)


def _attention(q, k, v, c, *, mode, tq, tk, off):
    b, t_q, width = q.shape
    t_k = k.shape[1]
    assert tk % tq == 0 and off % tk == 0 and t_k % tk == 0 and t_q % tq == 0
    assert off + t_q <= t_k
    pairs = width // LANES
    in_specs = [pl.BlockSpec((1, tq, LANES), lambda bi, hp, qi: (bi, qi, hp)),
                pl.BlockSpec((1, t_k, LANES), lambda bi, hp, qi: (bi, 0, hp)),
                pl.BlockSpec((1, t_k, LANES), lambda bi, hp, qi: (bi, 0, hp))]
    args = [q, k, v]
    if mode == "fox":
        in_specs.append(pl.BlockSpec((1, 1) + c.shape[2:], lambda bi, hp, qi: (bi, hp, 0, 0, 0)))
        args.append(c)
    return pl.pallas_call(
        functools.partial(_attn_kernel, mode=mode, tq=tq, tk=tk, off=off),
        grid=(b, pairs, t_q // tq),
        in_specs=in_specs,
        out_specs=pl.BlockSpec((1, tq, LANES), lambda bi, hp, qi: (bi, qi, hp)),
        out_shape=jax.ShapeDtypeStruct((b, t_q, width), BF16),
        compiler_params=pltpu.CompilerParams(
            dimension_semantics=("arbitrary", "arbitrary", "arbitrary"),
            vmem_limit_bytes=VMEM_LIMIT),
        name=f"attn_{mode}",
    )(*args)


def _key_bias_layout(c, n_heads, tk):
    b, _, t_k = c.shape
    c = c[:, :n_heads].reshape(b, n_heads // HEADS_PER_TILE, HEADS_PER_TILE, t_k // tk, tk)
    c = jnp.transpose(c, (0, 1, 3, 2, 4))
    return jnp.pad(c, ((0, 0), (0, 0), (0, 0), (0, 8 - HEADS_PER_TILE), (0, 0)))


def _post_kernel(*refs, nb, mix, final):
    if final:
        (x_ref, o_ref, qm_ref, mk_ref, mv_ref, wo_ref, g_ref, wgu_ref, wd_ref, gf_ref,
         y_ref, acc_ref) = refs
    else:
        (x_ref, o_ref, qm_ref, mk_ref, mv_ref, wo_ref, g_ref, wgu_ref, wd_ref,
         xo_ref, acc_ref) = refs
    tm = x_ref.shape[0]
    rb = tm // nb
    memw = qm_ref.shape[1]
    lane = lax.broadcasted_iota(jnp.int32, (rb, LANES), 1)

    om_rows = []
    for bi in range(nb):
        tiles = []
        for tl in range(memw // LANES):
            lanes = slice(tl * LANES, (tl + 1) * LANES)
            qp = qm_ref[bi * rb:(bi + 1) * rb, lanes]
            mk = mk_ref[bi, :, lanes]
            mv = mv_ref[bi, :, lanes]
            tile = None
            for hh in range(HEADS_PER_TILE):
                in_head = (lane >= hh * HEAD_DIM) & (lane < (hh + 1) * HEAD_DIM)
                qh = jnp.where(in_head, qp, jnp.zeros_like(qp))
                s = lax.dot_general(qh, mk, NT_DIMS, preferred_element_type=F32)
                p = jnp.exp(s - jnp.max(s, axis=1, keepdims=True))
                o = jnp.dot(p.astype(BF16), mv, preferred_element_type=F32)
                o = o / jnp.sum(p, axis=1, keepdims=True)
                tile = o if tile is None else jnp.where(in_head, o, tile)
            tiles.append(tile)
        om_rows.append(jnp.concatenate(tiles, axis=1))
    om = (om_rows[0] if nb == 1 else jnp.concatenate(om_rows, axis=0)).astype(BF16)

    x1 = (x_ref[...]
          + jnp.dot(o_ref[...], wo_ref[0:mix, :], preferred_element_type=F32)
          + jnp.dot(om, wo_ref[mix:mix + memw, :], preferred_element_type=F32))
    h = (_rms_scale(x1) * g_ref[...]).astype(BF16)
    ffc = wd_ref.shape[1]

    acc_ref[...] = x1

    def chunk(ci, _):
        gu = jnp.dot(h, wgu_ref[ci], preferred_element_type=F32)
        gate = gu[:, :ffc]
        act = (gate * jax.nn.sigmoid(gate) * gu[:, ffc:]).astype(BF16)
        acc_ref[...] += jnp.dot(act, wd_ref[ci], preferred_element_type=F32)
        return 0

    lax.fori_loop(0, wgu_ref.shape[0], chunk, 0)
    if final:
        y_ref[...] = _rms_scale(acc_ref[...]) * gf_ref[...]
    else:
        xo_ref[...] = acc_ref[...]


def _post(x, o_mix, qm, mk, mv, w_out, g_ffn, w_gu, w_down, g_final, *, tm, nb):
    rows, d = x.shape
    mix = o_mix.shape[1]
    memw = qm.shape[1]
    n_mem = mk.shape[1]
    final = g_final is not None
    rows_per_batch = rows // mk.shape[0]
    assert tm == nb * rows_per_batch or (nb == 1 and rows_per_batch % tm == 0)
    blocks_per_batch = max(rows_per_batch // tm, 1)
    row = lambda width: pl.BlockSpec((tm, width), lambda r: (r, 0))
    mem_spec = pl.BlockSpec((nb, n_mem, memw), lambda r: (r // blocks_per_batch, 0, 0))
    in_specs = [row(d), row(mix), row(memw), mem_spec, mem_spec,
                _const_spec(w_out.shape), _const_spec((1, d)),
                _const_spec(w_gu.shape), _const_spec(w_down.shape)]
    args = [x, o_mix, qm, mk, mv, w_out, g_ffn.reshape(1, d), w_gu, w_down]
    if final:
        in_specs.append(_const_spec((1, d)))
        args.append(g_final.reshape(1, d))
    return pl.pallas_call(
        functools.partial(_post_kernel, nb=nb, mix=mix, final=final),
        grid=(rows // tm,),
        in_specs=in_specs,
        out_specs=row(d),
        out_shape=jax.ShapeDtypeStruct((rows, d), F32),
        scratch_shapes=[pltpu.VMEM((tm, d), F32)],
        compiler_params=pltpu.CompilerParams(
            dimension_semantics=("arbitrary",), vmem_limit_bytes=VMEM_LIMIT),
        name="post_final" if final else "post",
    )(*args)


def _prep_w_in(w, mix, memw, n_forget):
    q, k, v = w[:, :mix], w[:, mix:2 * mix], w[:, 2 * mix:3 * mix]
    qm = w[:, 3 * mix + n_forget:]
    parts = [q * SCALE, k, v, qm * SCALE]
    if n_forget:
        parts.append(jnp.pad(w[:, 3 * mix:3 * mix + n_forget], ((0, 0), (0, LANES - n_forget))))
    return jnp.concatenate(parts, axis=1).astype(BF16)


def _ff_chunk(d_ff):
    for c in (512, 256, 128):
        if d_ff % c == 0:
            return c
    raise ValueError(f"d_ff={d_ff} is not a multiple of {LANES}")


def _prep_ffn(w_gate_up, w_down):
    d, two_ff = w_gate_up.shape
    d_ff = two_ff // 2
    ffc = _ff_chunk(d_ff)
    n = d_ff // ffc
    gate = w_gate_up[:, :d_ff].reshape(d, n, ffc)
    up = w_gate_up[:, d_ff:].reshape(d, n, ffc)
    w_gu = jnp.transpose(jnp.concatenate([gate, up], axis=2), (1, 0, 2)).astype(BF16)
    return w_gu, w_down.reshape(n, ffc, d).astype(BF16)


def _row_tile(rows, want):
    return want if rows % want == 0 else rows


def kernel(x_prompt, x_sample, mem_prompt, cache_fox_k, cache_fox_v, cache_fox_logf, cache_sb_k, cache_sb_v, cache_mem_k, cache_mem_v, g_mix, w_in_fox, b_f, w_in_sb, g_mem, w_mem_kv, w_out, g_ffn, w_gate_up, w_down, g_final):
    bp, seq, d = x_prompt.shape
    bs, dec_seq, _ = x_sample.shape
    n_mem = mem_prompt.shape[1]
    depth = g_mix.shape[0]
    n_heads = b_f.shape[1]
    mix = n_heads * HEAD_DIM
    memw = w_mem_kv.shape[2] // 2
    n_mem_heads = memw // HEAD_DIM
    past = cache_fox_k.shape[2]

    tk = 256
    tq_p = min(256, seq)
    tm_p = _row_tile(seq, 512)
    assert dec_seq <= tk and tk % dec_seq == 0 and past % tk == 0 and seq % tk == 0
    kv_len_s = past + tk
    kv_pad = tk - dec_seq

    xp = x_prompt.reshape(bp * seq, d)
    xs = x_sample.reshape(bs * dec_seq, d)

    mk_p, mv_p, mkb_p, mvb_p = _memory_kv(
        mem_prompt.reshape(bp * n_mem, d), g_mem, w_mem_kv.astype(BF16),
        tm=_row_tile(bp * n_mem, 512))

    fk_p, fv_p, fl_p, sk_p, sv_p = [], [], [], [], []
    fk_s, fv_s, fl_s, sk_s, sv_s = [], [], [], [], []
    yp = ys = None
    for i in range(depth):
        j = i // 2
        is_fox = i % 2 == 0
        if is_fox:
            w_in = _prep_w_in(w_in_fox[j], mix, memw, n_heads)
            bias = jnp.pad(b_f[j], (0, LANES - n_heads)).reshape(1, LANES)
        else:
            w_in = _prep_w_in(w_in_sb[j], mix, memw, 0)
            bias = None
        w_gu, w_dn = _prep_ffn(w_gate_up[i], w_down[i])
        w_o = w_out[i].astype(BF16)
        g_fin = g_final if i == depth - 1 else None

        outs = _project(xp, g_mix[i], w_in, bias, mix=mix, memw=memw, tm=tm_p)
        q, k, v, kb, vb, qm = outs[:6]
        shape_p = (bp, seq, mix)
        if is_fox:
            lf = outs[6].reshape(bp, seq, LANES)
            c = _key_bias_layout(_cumulative_logf(lf), n_heads, tk)
            o = _attention(q.reshape(shape_p), kb.reshape(shape_p), vb.reshape(shape_p), c,
                           mode="fox", tq=tq_p, tk=tk, off=0)
            fk_p.append(k.reshape(bp, seq, n_heads, HEAD_DIM))
            fv_p.append(v.reshape(bp, seq, n_heads, HEAD_DIM))
            fl_p.append(lf[:, :, :n_heads])
        else:
            o = _attention(q.reshape(shape_p), kb.reshape(shape_p), vb.reshape(shape_p), None,
                           mode="sb", tq=tq_p, tk=tk, off=0)
            sk_p.append(k.reshape(bp, seq, n_heads, HEAD_DIM))
            sv_p.append(v.reshape(bp, seq, n_heads, HEAD_DIM))
        xp_new = _post(xp, o.reshape(bp * seq, mix), qm,
                       mkb_p[i].reshape(bp, n_mem, memw), mvb_p[i].reshape(bp, n_mem, memw),
                       w_o, g_ffn[i], w_gu, w_dn, g_fin, tm=tm_p, nb=1)

        outs = _project(xs, g_mix[i], w_in, bias, mix=mix, memw=memw, tm=bs * dec_seq)
        q, k, v, kb, vb, qm = outs[:6]
        shape_s = (bs, dec_seq, mix)
        pad_new = lambda a: jnp.pad(a.reshape(shape_s), ((0, 0), (0, kv_pad), (0, 0)))
        if is_fox:
            ck, cv = cache_fox_k[j], cache_fox_v[j]
        else:
            ck, cv = cache_sb_k[j], cache_sb_v[j]
        k_all = jnp.concatenate([ck.reshape(bs, past, mix).astype(BF16), pad_new(kb)], axis=1)
        v_all = jnp.concatenate([cv.reshape(bs, past, mix).astype(BF16), pad_new(vb)], axis=1)
        if is_fox:
            lf = outs[6].reshape(bs, dec_seq, LANES)
            lf_all = jnp.concatenate(
                [jnp.pad(cache_fox_logf[j], ((0, 0), (0, 0), (0, LANES - n_heads))),
                 jnp.pad(lf, ((0, 0), (0, kv_pad), (0, 0)))], axis=1)
            c = _key_bias_layout(_cumulative_logf(lf_all), n_heads, tk)
            o = _attention(q.reshape(shape_s), k_all, v_all, c,
                           mode="fox", tq=dec_seq, tk=tk, off=past)
            fk_s.append(k.reshape(bs, dec_seq, n_heads, HEAD_DIM))
            fv_s.append(v.reshape(bs, dec_seq, n_heads, HEAD_DIM))
            fl_s.append(lf[:, :, :n_heads])
        else:
            o = _attention(q.reshape(shape_s), k_all, v_all, None,
                           mode="sb", tq=dec_seq, tk=tk, off=past)
            sk_s.append(k.reshape(bs, dec_seq, n_heads, HEAD_DIM))
            sv_s.append(v.reshape(bs, dec_seq, n_heads, HEAD_DIM))
        xs_new = _post(xs, o.reshape(bs * dec_seq, mix), qm,
                       cache_mem_k[i].reshape(bs, n_mem, memw).astype(BF16),
                       cache_mem_v[i].reshape(bs, n_mem, memw).astype(BF16),
                       w_o, g_ffn[i], w_gu, w_dn, g_fin, tm=bs * dec_seq, nb=bs)
        xp, xs = xp_new, xs_new

    mem_shape = (depth, bp, n_mem, n_mem_heads, HEAD_DIM)
    return (xp.reshape(bp, seq, d), xs.reshape(bs, dec_seq, d),
            jnp.stack(fk_p), jnp.stack(fv_p), jnp.stack(fl_p), jnp.stack(sk_p), jnp.stack(sv_p),
            mk_p.reshape(mem_shape), mv_p.reshape(mem_shape),
            jnp.stack(fk_s), jnp.stack(fv_s), jnp.stack(fl_s), jnp.stack(sk_s), jnp.stack(sv_s))
```

```python
import functools

import jax
import jax.numpy as jnp
from jax import lax
from jax.experimental import pallas as pl
from jax.experimental.pallas import tpu as pltpu

HEAD_DIM = 64
LANES = 128
HEADS_PER_TILE = LANES // HEAD_DIM
FORGET_ROWS = 16
EPS = 1e-6
LOG2E = 1.4426950408889634
QK_SCALE = HEAD_DIM ** -0.5 * LOG2E
NEG = -1e30
VMEM_LIMIT = 56 * 1024 * 1024

F32 = jnp.float32
BF16 = jnp.bfloat16
NT_DIMS = (((1,), (1,)), ((), ()))


def _rms_scale(x):
    return x * lax.rsqrt(jnp.mean(x * x, axis=-1, keepdims=True) + EPS)


def _const_spec(shape):
    return pl.BlockSpec(shape, lambda *_: (0,) * len(shape), pipeline_mode=pl.Buffered(1))


def _head_mask(rows, hh):
    lane = lax.broadcasted_iota(jnp.int32, (rows, LANES), 1)
    return (lane >= hh * HEAD_DIM) & (lane < (hh + 1) * HEAD_DIM)


def _proj_kernel(*refs, mix, tk, has_forget, n_aliased):
    refs = list(refs)
    x_ref, g_ref, wn_ref, wt_ref = refs[:4]
    del refs[:4]
    if has_forget:
        bf_ref = refs.pop(0)
    del refs[:n_aliased]
    q_ref, qm_ref, kt_ref, vt_ref, ktb_ref, vtb_ref = refs[:6]
    if has_forget:
        lft_ref = refs[6]
    tm = x_ref.shape[0]
    hn = (_rms_scale(x_ref[...]) * g_ref[...]).astype(BF16)
    qq = jnp.dot(hn, wn_ref[...], preferred_element_type=F32) * QK_SCALE
    q_ref[...] = qq[:, :mix].astype(BF16)
    qm_ref[...] = qq[:, mix:].astype(BF16)
    t = lax.dot_general(wt_ref[...], hn, NT_DIMS, preferred_element_type=F32)
    kt = t[:mix]
    vt = t[mix:2 * mix]
    kt_ref[0, 0] = kt
    vt_ref[0, 0] = vt
    for c in range(tm // tk):
        ktb_ref[0, c] = kt[:, c * tk:(c + 1) * tk].astype(BF16)
        vtb_ref[0, c] = vt[:, c * tk:(c + 1) * tk].astype(BF16)
    if has_forget:
        f = t[2 * mix:] + bf_ref[...]
        lft_ref[0] = jnp.minimum(f, 0.0) - jnp.log(1.0 + jnp.exp(-jnp.abs(f)))


def _project(x, g, wn, wt, bias, *, batch, mix, memw, tm, tk, slot=0, n_slots=1, stacked=None):
    rows, d = x.shape
    t_len = rows // batch
    per_batch = t_len // tm
    has_forget = bias is not None
    row = lambda width: pl.BlockSpec((tm, width), lambda r: (r, 0))
    feat = pl.BlockSpec((1, FORGET_ROWS, tm), lambda r: (r // per_batch, 0, r % per_batch))
    slotted = pl.BlockSpec((1, 1, mix, tm), lambda r: (slot, r // per_batch, 0, r % per_batch))
    blocked = pl.BlockSpec((1, tm // tk, mix, tk), lambda r: (r // per_batch, r % per_batch, 0, 0))
    in_specs = [row(d), _const_spec((1, d)), _const_spec(wn.shape), _const_spec(wt.shape)]
    args = [x, g.reshape(1, d), wn, wt]
    if has_forget:
        in_specs.append(_const_spec((FORGET_ROWS, 1)))
        args.append(bias)
    aliases = {}
    if stacked is not None:
        for a in stacked:
            aliases[len(args)] = 2 + len(aliases)
            in_specs.append(pl.BlockSpec(memory_space=pl.ANY))
            args.append(a)
    out_shape = [jax.ShapeDtypeStruct((rows, mix), BF16),
                 jax.ShapeDtypeStruct((rows, memw), BF16),
                 jax.ShapeDtypeStruct((n_slots, batch, mix, t_len), F32),
                 jax.ShapeDtypeStruct((n_slots, batch, mix, t_len), F32),
                 jax.ShapeDtypeStruct((batch, t_len // tk, mix, tk), BF16),
                 jax.ShapeDtypeStruct((batch, t_len // tk, mix, tk), BF16)]
    out_specs = [row(mix), row(memw), slotted, slotted, blocked, blocked]
    if has_forget:
        out_shape.append(jax.ShapeDtypeStruct((batch, FORGET_ROWS, t_len), F32))
        out_specs.append(feat)
    return pl.pallas_call(
        functools.partial(_proj_kernel, mix=mix, tk=tk, has_forget=has_forget,
                          n_aliased=len(aliases)),
        grid=(rows // tm,),
        in_specs=in_specs, out_specs=out_specs, out_shape=out_shape,
        input_output_aliases=aliases,
        compiler_params=pltpu.CompilerParams(
            dimension_semantics=("arbitrary",), vmem_limit_bytes=VMEM_LIMIT),
        name="proj_fox" if has_forget else "proj_sb",
    )(*args)


def _memkv_kernel(x_ref, g_ref, wt_ref, kt_ref, vt_ref, *, memw):
    hn = (_rms_scale(x_ref[0]) * g_ref[0]).astype(BF16)
    kvt = lax.dot_general(wt_ref[0], hn, NT_DIMS, preferred_element_type=F32)
    kt_ref[0, 0] = kvt[:memw]
    vt_ref[0, 0] = kvt[memw:]


def _memory_kv(mem, g_mem, wt_mem_kv):
    batch, n_mem, d = mem.shape
    depth, two_memw, _ = wt_mem_kv.shape
    memw = two_memw // 2
    out_spec = pl.BlockSpec((1, 1, memw, n_mem), lambda l, b: (l, b, 0, 0))
    return pl.pallas_call(
        functools.partial(_memkv_kernel, memw=memw),
        grid=(depth, batch),
        in_specs=[pl.BlockSpec((1, n_mem, d), lambda l, b: (b, 0, 0)),
                  pl.BlockSpec((1, 1, d), lambda l, b: (l, 0, 0)),
                  pl.BlockSpec((1, two_memw, d), lambda l, b: (l, 0, 0))],
        out_specs=[out_spec] * 2,
        out_shape=[jax.ShapeDtypeStruct((depth, batch, memw, n_mem), F32)] * 2,
        compiler_params=pltpu.CompilerParams(
            dimension_semantics=("arbitrary", "arbitrary"), vmem_limit_bytes=VMEM_LIMIT),
        name="memkv",
    )(mem, g_mem.reshape(depth, 1, d), wt_mem_kv)


def _split3(x):
    hi = x.astype(BF16)
    r1 = x - hi.astype(F32)
    mid = r1.astype(BF16)
    lo = (r1 - mid.astype(F32)).astype(BF16)
    return hi, mid, lo


def _cumsum_kernel(lf_ref, c_ref, *, t_len):
    r = lax.broadcasted_iota(jnp.int32, (LANES, LANES), 0)
    c = lax.broadcasted_iota(jnp.int32, (LANES, LANES), 1)
    upper = (r <= c).astype(BF16)
    carry = jnp.zeros((lf_ref.shape[1], 1), F32)
    for ch in range(t_len // LANES):
        cs = carry
        for part in _split3(lf_ref[0, :, ch * LANES:(ch + 1) * LANES]):
            cs = cs + jnp.dot(part, upper, preferred_element_type=F32)
        c_ref[0, :, ch * LANES:(ch + 1) * LANES] = cs * LOG2E
        carry = cs[:, LANES - 1:LANES]


def _cumulative_logf(lft):
    b, rows, t_len = lft.shape
    spec = pl.BlockSpec((1, rows, t_len), lambda i: (i, 0, 0))
    return pl.pallas_call(
        functools.partial(_cumsum_kernel, t_len=t_len),
        grid=(b,), in_specs=[spec], out_specs=spec,
        out_shape=jax.ShapeDtypeStruct((b, rows, t_len), F32),
        compiler_params=pltpu.CompilerParams(
            dimension_semantics=("arbitrary",), vmem_limit_bytes=VMEM_LIMIT),
        name="cumsum_logf",
    )(lft)


def _key_bias_layout(c, n_heads, tk):
    b, _, t_k = c.shape
    c = c[:, :n_heads].reshape(b, n_heads // HEADS_PER_TILE, HEADS_PER_TILE, t_k // tk, tk)
    c = jnp.transpose(c, (0, 1, 3, 2, 4))
    return jnp.pad(c, ((0, 0), (0, 0), (0, 0), (0, 8 - HEADS_PER_TILE), (0, 0)))


def _attn_kernel(*refs, mode, tq, tk, ks, past, cached):
    refs = list(refs)
    q_ref = refs.pop(0)
    if cached:
        ckt_ref, cvt_ref, nkt_ref, nvt_ref = refs[:4]
        refs = refs[4:]
    else:
        kt_ref, vt_ref = refs[:2]
        refs = refs[2:]
    c_ref = refs.pop(0) if mode == "fox" else None
    (o_ref,) = refs

    spb = ks // tk
    if cached:
        n_super = past // ks
        diag_blocks = [past // tk]

        def kv_block(j):
            if j == past // tk:
                return nkt_ref[0], nvt_ref[0]
            cols = slice(j * tk, (j + 1) * tk)
            return ckt_ref[0, 0, :, cols].astype(BF16), cvt_ref[0, 0, :, cols].astype(BF16)
    else:
        n_super = pl.program_id(2) * (tq // ks)
        diag_blocks = [n_super * spb + t for t in range(tq // tk)]

        def kv_block(j):
            return kt_ref[0, j], vt_ref[0, j]

    q = q_ref[0]
    qh = [jnp.where(_head_mask(tq, hh), q, jnp.zeros_like(q))
          for hh in range(HEADS_PER_TILE)]

    def visible_mask(width, koff, strict):
        r = lax.broadcasted_iota(jnp.int32, (tq, width), 0)
        c = lax.broadcasted_iota(jnp.int32, (tq, width), 1) + koff
        return c < r if strict else c <= r

    if mode == "fox":
        def update(state, blocks, masked):
            kv = [kv_block(j) for j in blocks]
            cb = [c_ref[0, 0, j] for j in blocks]
            cb = cb[0] if len(cb) == 1 else jnp.concatenate(cb, axis=1)
            if masked:
                visible = visible_mask(len(blocks) * tk, 0, strict=False)
            out = []
            for hh in range(HEADS_PER_TILE):
                m, l, acc = state[3 * hh:3 * hh + 3]
                s = [jnp.dot(qh[hh], kt, preferred_element_type=F32) for kt, _ in kv]
                s = s[0] if len(s) == 1 else jnp.concatenate(s, axis=1)
                s = s - cb[hh:hh + 1, :]
                if masked:
                    s = jnp.where(visible, s, NEG)
                m_new = jnp.maximum(m, jnp.max(s, axis=1, keepdims=True))
                alpha = jnp.exp2(m - m_new)
                p = jnp.exp2(s - m_new)
                l = alpha * l + jnp.sum(p, axis=1, keepdims=True)
                p = p.astype(BF16)
                acc = alpha * acc
                for t, (_, vt) in enumerate(kv):
                    acc = acc + lax.dot_general(p[:, t * tk:(t + 1) * tk], vt, NT_DIMS,
                                                preferred_element_type=F32)
                out += [m_new, l, acc]
            return tuple(out)

        state = (jnp.full((tq, 1), NEG, F32), jnp.zeros((tq, 1), F32),
                 jnp.zeros((tq, LANES), F32)) * HEADS_PER_TILE
        if cached:
            if n_super:
                state = update(state, list(range(n_super * spb)), False)
        else:
            state = lax.fori_loop(
                0, n_super,
                lambda j, st: update(st, [j * spb + t for t in range(spb)], False), state)
        state = update(state, diag_blocks, True)
        head_out = [state[3 * hh + 2] / state[3 * hh + 1] for hh in range(HEADS_PER_TILE)]
    else:
        r = lax.broadcasted_iota(jnp.int32, (tk, tk), 0)
        c = lax.broadcasted_iota(jnp.int32, (tk, tk), 1)
        later = (r > c).astype(BF16)

        def update(state, blocks, masked):
            kv = [kv_block(j) for j in blocks]
            if masked:
                visible = [visible_mask(tk, t * tk, strict=True) for t in range(len(blocks))]
            out = []
            for hh in range(HEADS_PER_TILE):
                run, acc = state[2 * hh:2 * hh + 2]
                logits, block_sums = [], []
                for t, (kt, _) in enumerate(kv):
                    z = jnp.dot(qh[hh], kt, preferred_element_type=F32)
                    u = -(jnp.maximum(z, 0.0) + jnp.log2(1.0 + jnp.exp2(-jnp.abs(z))))
                    if masked:
                        u = jnp.where(visible[t], u, 0.0)
                    hi = u.astype(BF16)
                    lo = (u - hi.astype(F32)).astype(BF16)
                    rest = (jnp.dot(hi, later, preferred_element_type=F32)
                            + jnp.dot(lo, later, preferred_element_type=F32))
                    logits.append(z + u + rest)
                    block_sums.append(jnp.sum(u, axis=1, keepdims=True))
                for t in reversed(range(len(kv))):
                    a = jnp.exp2(logits[t] + run)
                    if masked:
                        a = jnp.where(visible[t], a, 0.0)
                    acc = acc + lax.dot_general(a.astype(BF16), kv[t][1], NT_DIMS,
                                                preferred_element_type=F32)
                    run = run + block_sums[t]
                out += [run, acc]
            return tuple(out)

        state = (jnp.zeros((tq, 1), F32), jnp.zeros((tq, LANES), F32)) * HEADS_PER_TILE
        state = update(state, diag_blocks, True)
        if cached:
            if n_super:
                state = update(state, list(range(n_super * spb)), False)
        else:
            state = lax.fori_loop(
                0, n_super,
                lambda t, st: update(st, [(n_super - 1 - t) * spb + b for b in range(spb)], False),
                state)
        head_out = [state[2 * hh + 1] for hh in range(HEADS_PER_TILE)]

    o_ref[0] = jnp.where(_head_mask(tq, 0), head_out[0], head_out[1]).astype(o_ref.dtype)


def _attention(q, kv, c, *, mode, tq, tk, ks, past=0, cache_slot=0):
    b, t_q, width = q.shape
    cached = len(kv) == 4
    pairs = width // LANES
    q_spec = pl.BlockSpec((1, tq, LANES), lambda bi, hp, qi: (bi, qi, hp))
    if cached:
        assert t_q == tq and tq <= tk and past % ks == 0 and ks % tk == 0
        old = pl.BlockSpec((1, 1, LANES, past), lambda bi, hp, qi: (cache_slot, bi, hp, 0))
        new = pl.BlockSpec((1, LANES, tk), lambda bi, hp, qi: (bi, hp, 0))
        kv_specs = [old, old, new, new]
    else:
        assert t_q % tq == 0 and tq % ks == 0 and ks % tk == 0 and past == 0
        blocked = pl.BlockSpec((1, t_q // tk, LANES, tk), lambda bi, hp, qi: (bi, 0, hp, 0))
        kv_specs = [blocked, blocked]
    in_specs = [q_spec] + kv_specs
    args = [q, *kv]
    if mode == "fox":
        in_specs.append(pl.BlockSpec((1, 1) + c.shape[2:], lambda bi, hp, qi: (bi, hp, 0, 0, 0)))
        args.append(c)
    return pl.pallas_call(
        functools.partial(_attn_kernel, mode=mode, tq=tq, tk=tk, ks=ks, past=past, cached=cached),
        grid=(b, pairs, t_q // tq),
        in_specs=in_specs,
        out_specs=q_spec,
        out_shape=jax.ShapeDtypeStruct((b, t_q, width), BF16),
        compiler_params=pltpu.CompilerParams(
            dimension_semantics=("arbitrary", "arbitrary", "arbitrary"),
            vmem_limit_bytes=VMEM_LIMIT),
        name=f"attn_{mode}",
    )(*args)


def _post_kernel(*refs, nb, mix, final):
    if final:
        (x_ref, o_ref, qm_ref, mkt_ref, mvt_ref, wo_ref, g_ref, wgu_ref, wd_ref, gf_ref,
         y_ref, acc_ref) = refs
    else:
        (x_ref, o_ref, qm_ref, mkt_ref, mvt_ref, wo_ref, g_ref, wgu_ref, wd_ref,
         xo_ref, acc_ref) = refs
    tm = x_ref.shape[0]
    rb = tm // nb
    memw = qm_ref.shape[1]

    om_rows = []
    for bi in range(nb):
        tiles = []
        for tl in range(memw // LANES):
            feats = slice(tl * LANES, (tl + 1) * LANES)
            qp = qm_ref[bi * rb:(bi + 1) * rb, feats]
            mkt = mkt_ref[0, bi, feats, :].astype(BF16)
            mvt = mvt_ref[0, bi, feats, :].astype(BF16)
            tile = None
            for hh in range(HEADS_PER_TILE):
                in_head = _head_mask(rb, hh)
                qh = jnp.where(in_head, qp, jnp.zeros_like(qp))
                s = jnp.dot(qh, mkt, preferred_element_type=F32)
                p = jnp.exp2(s - jnp.max(s, axis=1, keepdims=True))
                o = lax.dot_general(p.astype(BF16), mvt, NT_DIMS, preferred_element_type=F32)
                o = o / jnp.sum(p, axis=1, keepdims=True)
                tile = o if tile is None else jnp.where(in_head, o, tile)
            tiles.append(tile)
        om_rows.append(jnp.concatenate(tiles, axis=1))
    om = (om_rows[0] if nb == 1 else jnp.concatenate(om_rows, axis=0)).astype(BF16)

    x1 = (x_ref[...]
          + jnp.dot(o_ref[...], wo_ref[0:mix, :], preferred_element_type=F32)
          + jnp.dot(om, wo_ref[mix:mix + memw, :], preferred_element_type=F32))
    h = (_rms_scale(x1) * g_ref[...]).astype(BF16)
    ffc = wd_ref.shape[1]

    acc_ref[...] = x1

    def chunk(ci, _):
        gu = jnp.dot(h, wgu_ref[ci], preferred_element_type=F32)
        gate = gu[:, :ffc]
        act = (gate * jax.nn.sigmoid(gate) * gu[:, ffc:]).astype(BF16)
        acc_ref[...] += jnp.dot(act, wd_ref[ci], preferred_element_type=F32)
        return 0

    lax.fori_loop(0, wgu_ref.shape[0], chunk, 0)
    if final:
        y_ref[...] = _rms_scale(acc_ref[...]) * gf_ref[...]
    else:
        xo_ref[...] = acc_ref[...]


def _post(x, o_mix, qm, mkt, mvt, layer, w_out, g_ffn, w_gu, w_down, g_final, *, tm, nb):
    rows, d = x.shape
    mix = o_mix.shape[1]
    memw = qm.shape[1]
    n_mem = mkt.shape[3]
    final = g_final is not None
    rows_per_batch = rows // mkt.shape[1]
    assert tm == nb * rows_per_batch or (nb == 1 and rows_per_batch % tm == 0)
    blocks_per_batch = max(rows_per_batch // tm, 1)
    row = lambda width: pl.BlockSpec((tm, width), lambda r: (r, 0))
    mem_spec = pl.BlockSpec((1, nb, memw, n_mem), lambda r: (layer, r // blocks_per_batch, 0, 0))
    in_specs = [row(d), row(mix), row(memw), mem_spec, mem_spec,
                _const_spec(w_out.shape), _const_spec((1, d)),
                _const_spec(w_gu.shape), _const_spec(w_down.shape)]
    args = [x, o_mix, qm, mkt, mvt, w_out, g_ffn.reshape(1, d), w_gu, w_down]
    if final:
        in_specs.append(_const_spec((1, d)))
        args.append(g_final.reshape(1, d))
    return pl.pallas_call(
        functools.partial(_post_kernel, nb=nb, mix=mix, final=final),
        grid=(rows // tm,),
        in_specs=in_specs,
        out_specs=row(d),
        out_shape=jax.ShapeDtypeStruct((rows, d), F32),
        scratch_shapes=[pltpu.VMEM((tm, d), F32)],
        compiler_params=pltpu.CompilerParams(
            dimension_semantics=("arbitrary",), vmem_limit_bytes=VMEM_LIMIT),
        name="post_final" if final else "post",
    )(*args)


def _prep_w_in(w, mix, n_forget):
    wn = jnp.concatenate([w[:, :mix], w[:, 3 * mix + n_forget:]], axis=1)
    wt = w[:, mix:3 * mix + n_forget]
    if n_forget:
        wt = jnp.pad(wt, ((0, 0), (0, FORGET_ROWS - n_forget)))
    return wn.astype(BF16), wt.T.astype(BF16)


def _ff_chunk(d_ff):
    for c in (512, 256, 128):
        if d_ff % c == 0:
            return c
    raise ValueError(f"d_ff={d_ff} is not a multiple of {LANES}")


def _prep_ffn(w_gate_up, w_down):
    d, two_ff = w_gate_up.shape
    d_ff = two_ff // 2
    ffc = _ff_chunk(d_ff)
    n = d_ff // ffc
    gate = w_gate_up[:, :d_ff].reshape(d, n, ffc)
    up = w_gate_up[:, d_ff:].reshape(d, n, ffc)
    w_gu = jnp.transpose(jnp.concatenate([gate, up], axis=2), (1, 0, 2)).astype(BF16)
    return w_gu, w_down.reshape(n, ffc, d).astype(BF16)


def _feature_major(cache):
    *lead, t, h, hd = cache.shape
    return jnp.moveaxis(cache, -3, -1).reshape(*lead, h * hd, t)


def _token_major(xt):
    *lead, width, t = xt.shape
    return jnp.moveaxis(xt.reshape(*lead, width // HEAD_DIM, HEAD_DIM, t), -1, -3)


def kernel(x_prompt, x_sample, mem_prompt, cache_fox_k, cache_fox_v, cache_fox_logf, cache_sb_k, cache_sb_v, cache_mem_k, cache_mem_v, g_mix, w_in_fox, b_f, w_in_sb, g_mem, w_mem_kv, w_out, g_ffn, w_gate_up, w_down, g_final):
    bp, seq, d = x_prompt.shape
    bs, dec_seq, _ = x_sample.shape
    n_mem = mem_prompt.shape[1]
    depth = g_mix.shape[0]
    n_heads = b_f.shape[1]
    mix = n_heads * HEAD_DIM
    memw = w_mem_kv.shape[2] // 2
    n_mem_heads = memw // HEAD_DIM
    past = cache_fox_k.shape[2]

    tk = 256
    ks = 512
    tq_p = ks
    tm_p = 512
    rows_s = bs * dec_seq
    assert seq % tq_p == 0 and seq % tm_p == 0 and rows_s % tk == 0
    assert dec_seq <= tk and past % ks == 0

    xp = x_prompt.reshape(bp * seq, d)
    xs = x_sample.reshape(rows_s, d)

    mkt_p, mvt_p = _memory_kv(mem_prompt, g_mem, jnp.transpose(w_mem_kv, (0, 2, 1)).astype(BF16))

    def new_keys(xt):
        xt = jnp.transpose(xt.reshape(mix, bs, dec_seq), (1, 0, 2))
        return jnp.pad(xt, ((0, 0), (0, 0), (0, tk - dec_seq))).astype(BF16)

    caches = {"fox": (_feature_major(cache_fox_k), _feature_major(cache_fox_v)),
              "sb": (_feature_major(cache_sb_k), _feature_major(cache_sb_v))}
    cmkt, cmvt = _feature_major(cache_mem_k), _feature_major(cache_mem_v)
    n_layers = {"fox": (depth + 1) // 2, "sb": depth // 2}
    kv_p = {"fox": None, "sb": None}
    fl_p = []
    fk_s, fv_s, fl_s, sk_s, sv_s = [], [], [], [], []
    for i in range(depth):
        j = i // 2
        is_fox = i % 2 == 0
        mode = "fox" if is_fox else "sb"
        if is_fox:
            wn, wt = _prep_w_in(w_in_fox[j], mix, n_heads)
            bias = jnp.pad(b_f[j], (0, FORGET_ROWS - n_heads)).reshape(FORGET_ROWS, 1)
        else:
            wn, wt = _prep_w_in(w_in_sb[j], mix, 0)
            bias = None
        w_gu, w_dn = _prep_ffn(w_gate_up[i], w_down[i])
        w_o = w_out[i].astype(BF16)
        g_fin = g_final if i == depth - 1 else None

        outs = _project(xp, g_mix[i], wn, wt, bias, batch=bp, mix=mix, memw=memw, tm=tm_p, tk=tk,
                        slot=j, n_slots=n_layers[mode], stacked=kv_p[mode])
        q, qm, kt, vt, ktb, vtb = outs[:6]
        kv_p[mode] = (kt, vt)
        c = None
        if is_fox:
            lft = outs[6]
            c = _key_bias_layout(_cumulative_logf(lft), n_heads, tk)
            fl_p.append(jnp.transpose(lft[:, :n_heads], (0, 2, 1)))
        o = _attention(q.reshape(bp, seq, mix), (ktb, vtb), c, mode=mode, tq=tq_p, tk=tk, ks=ks)
        xp_new = _post(xp, o.reshape(bp * seq, mix), qm, mkt_p, mvt_p, i,
                       w_o, g_ffn[i], w_gu, w_dn, g_fin, tm=tm_p, nb=1)

        outs = _project(xs, g_mix[i], wn, wt, bias, batch=1, mix=mix, memw=memw, tm=rows_s, tk=tk)
        q, qm, kt, vt = outs[:4]
        c = None
        if is_fox:
            lft = outs[6]
            lft_s = jnp.transpose(lft.reshape(FORGET_ROWS, bs, dec_seq), (1, 0, 2))
            lft_past = jnp.pad(jnp.transpose(cache_fox_logf[j], (0, 2, 1)),
                               ((0, 0), (0, FORGET_ROWS - n_heads), (0, 0)))
            lft_all = jnp.concatenate(
                [lft_past, jnp.pad(lft_s, ((0, 0), (0, 0), (0, tk - dec_seq)))], axis=2)
            c = _key_bias_layout(_cumulative_logf(lft_all), n_heads, tk)
            fl_s.append(jnp.transpose(lft_s[:, :n_heads], (0, 2, 1)))
        new_shape = (n_heads, HEAD_DIM, bs, dec_seq)
        k_new = jnp.transpose(kt.reshape(new_shape), (2, 3, 0, 1))
        v_new = jnp.transpose(vt.reshape(new_shape), (2, 3, 0, 1))
        if is_fox:
            fk_s.append(k_new)
            fv_s.append(v_new)
        else:
            sk_s.append(k_new)
            sv_s.append(v_new)
        o = _attention(q.reshape(bs, dec_seq, mix),
                       (*caches[mode], new_keys(kt), new_keys(vt)), c,
                       mode=mode, tq=dec_seq, tk=tk, ks=ks, past=past, cache_slot=j)
        xs_new = _post(xs, o.reshape(rows_s, mix), qm, cmkt, cmvt, i,
                       w_o, g_ffn[i], w_gu, w_dn, g_fin, tm=rows_s, nb=bs)
        xp, xs = xp_new, xs_new

    return (xp.reshape(bp, seq, d), xs.reshape(bs, dec_seq, d),
            _token_major(kv_p["fox"][0]), _token_major(kv_p["fox"][1]), jnp.stack(fl_p),
            _token_major(kv_p["sb"][0]), _token_major(kv_p["sb"][1]),
            _token_major(mkt_p), _token_major(mvt_p),
            jnp.stack(fk_s), jnp.stack(fv_s), jnp.stack(fl_s), jnp.stack(sk_s), jnp.stack(sv_s))
```

```python
import functools

import jax
import jax.numpy as jnp
from jax import lax
from jax.experimental import pallas as pl
from jax.experimental.pallas import tpu as pltpu

HEAD_DIM = 64
LANES = 128
HEADS_PER_TILE = LANES // HEAD_DIM
FORGET_ROWS = 16
EPS = 1e-6
LOG2E = 1.4426950408889634
QK_SCALE = HEAD_DIM ** -0.5 * LOG2E
NEG = -1e30
DEAD_LOG2 = 160.0
VMEM_LIMIT = 56 * 1024 * 1024

F32 = jnp.float32
BF16 = jnp.bfloat16
NT_DIMS = (((1,), (1,)), ((), ()))


def _rms_scale(x):
    return x * lax.rsqrt(jnp.mean(x * x, axis=-1, keepdims=True) + EPS)


def _const_spec(shape):
    return pl.BlockSpec(shape, lambda *_: (0,) * len(shape), pipeline_mode=pl.Buffered(1))


def _head_mask(rows, hh):
    lane = lax.broadcasted_iota(jnp.int32, (rows, LANES), 1)
    return (lane >= hh * HEAD_DIM) & (lane < (hh + 1) * HEAD_DIM)


def _proj_kernel(*refs, mix, tk, has_forget, n_aliased):
    refs = list(refs)
    x_ref, g_ref, wn_ref, wt_ref = refs[:4]
    del refs[:4]
    if has_forget:
        bf_ref = refs.pop(0)
    del refs[:n_aliased]
    q_ref, qm_ref, kt_ref, vt_ref, ktb_ref, vtb_ref = refs[:6]
    if has_forget:
        lft_ref = refs[6]
    tm = x_ref.shape[0]
    hn = (_rms_scale(x_ref[...]) * g_ref[...]).astype(BF16)
    qq = jnp.dot(hn, wn_ref[...], preferred_element_type=F32) * QK_SCALE
    q_ref[...] = qq[:, :mix].astype(BF16)
    qm_ref[...] = qq[:, mix:].astype(BF16)
    t = lax.dot_general(wt_ref[...], hn, NT_DIMS, preferred_element_type=F32)
    kt = t[:mix]
    vt = t[mix:2 * mix]
    kt_ref[0, 0] = kt
    vt_ref[0, 0] = vt
    for c in range(tm // tk):
        ktb_ref[0, c] = kt[:, c * tk:(c + 1) * tk].astype(BF16)
        vtb_ref[0, c] = vt[:, c * tk:(c + 1) * tk].astype(BF16)
    if has_forget:
        f = t[2 * mix:] + bf_ref[...]
        lft_ref[0] = jnp.minimum(f, 0.0) - jnp.log(1.0 + jnp.exp(-jnp.abs(f)))


def _project(x, g, wn, wt, bias, *, batch, mix, memw, tm, tk, slot=0, n_slots=1, stacked=None):
    rows, d = x.shape
    t_len = rows // batch
    per_batch = t_len // tm
    has_forget = bias is not None
    row = lambda width: pl.BlockSpec((tm, width), lambda r: (r, 0))
    feat = pl.BlockSpec((1, FORGET_ROWS, tm), lambda r: (r // per_batch, 0, r % per_batch))
    slotted = pl.BlockSpec((1, 1, mix, tm), lambda r: (slot, r // per_batch, 0, r % per_batch))
    blocked = pl.BlockSpec((1, tm // tk, mix, tk), lambda r: (r // per_batch, r % per_batch, 0, 0))
    in_specs = [row(d), _const_spec((1, d)), _const_spec(wn.shape), _const_spec(wt.shape)]
    args = [x, g.reshape(1, d), wn, wt]
    if has_forget:
        in_specs.append(_const_spec((FORGET_ROWS, 1)))
        args.append(bias)
    aliases = {}
    if stacked is not None:
        for a in stacked:
            aliases[len(args)] = 2 + len(aliases)
            in_specs.append(pl.BlockSpec(memory_space=pl.ANY))
            args.append(a)
    out_shape = [jax.ShapeDtypeStruct((rows, mix), BF16),
                 jax.ShapeDtypeStruct((rows, memw), BF16),
                 jax.ShapeDtypeStruct((n_slots, batch, mix, t_len), F32),
                 jax.ShapeDtypeStruct((n_slots, batch, mix, t_len), F32),
                 jax.ShapeDtypeStruct((batch, t_len // tk, mix, tk), BF16),
                 jax.ShapeDtypeStruct((batch, t_len // tk, mix, tk), BF16)]
    out_specs = [row(mix), row(memw), slotted, slotted, blocked, blocked]
    if has_forget:
        out_shape.append(jax.ShapeDtypeStruct((batch, FORGET_ROWS, t_len), F32))
        out_specs.append(feat)
    return pl.pallas_call(
        functools.partial(_proj_kernel, mix=mix, tk=tk, has_forget=has_forget,
                          n_aliased=len(aliases)),
        grid=(rows // tm,),
        in_specs=in_specs, out_specs=out_specs, out_shape=out_shape,
        input_output_aliases=aliases,
        compiler_params=pltpu.CompilerParams(
            dimension_semantics=("arbitrary",), vmem_limit_bytes=VMEM_LIMIT),
        name="proj_fox" if has_forget else "proj_sb",
    )(*args)


def _memkv_kernel(x_ref, g_ref, wt_ref, kt_ref, vt_ref, *, memw):
    hn = (_rms_scale(x_ref[0]) * g_ref[0]).astype(BF16)
    kvt = lax.dot_general(wt_ref[0], hn, NT_DIMS, preferred_element_type=F32)
    kt_ref[0, 0] = kvt[:memw]
    vt_ref[0, 0] = kvt[memw:]


def _memory_kv(mem, g_mem, wt_mem_kv):
    batch, n_mem, d = mem.shape
    depth, two_memw, _ = wt_mem_kv.shape
    memw = two_memw // 2
    out_spec = pl.BlockSpec((1, 1, memw, n_mem), lambda l, b: (l, b, 0, 0))
    return pl.pallas_call(
        functools.partial(_memkv_kernel, memw=memw),
        grid=(depth, batch),
        in_specs=[pl.BlockSpec((1, n_mem, d), lambda l, b: (b, 0, 0)),
                  pl.BlockSpec((1, 1, d), lambda l, b: (l, 0, 0)),
                  pl.BlockSpec((1, two_memw, d), lambda l, b: (l, 0, 0))],
        out_specs=[out_spec] * 2,
        out_shape=[jax.ShapeDtypeStruct((depth, batch, memw, n_mem), F32)] * 2,
        compiler_params=pltpu.CompilerParams(
            dimension_semantics=("arbitrary", "arbitrary"), vmem_limit_bytes=VMEM_LIMIT),
        name="memkv",
    )(mem, g_mem.reshape(depth, 1, d), wt_mem_kv)


def _split3(x):
    hi = x.astype(BF16)
    r1 = x - hi.astype(F32)
    mid = r1.astype(BF16)
    lo = (r1 - mid.astype(F32)).astype(BF16)
    return hi, mid, lo


def _cumsum_kernel(lf_ref, c_ref, *, t_len):
    r = lax.broadcasted_iota(jnp.int32, (LANES, LANES), 0)
    c = lax.broadcasted_iota(jnp.int32, (LANES, LANES), 1)
    upper = (r <= c).astype(BF16)
    carry = jnp.zeros((lf_ref.shape[1], 1), F32)
    for ch in range(t_len // LANES):
        cs = carry
        for part in _split3(lf_ref[0, :, ch * LANES:(ch + 1) * LANES]):
            cs = cs + jnp.dot(part, upper, preferred_element_type=F32)
        c_ref[0, :, ch * LANES:(ch + 1) * LANES] = cs * LOG2E
        carry = cs[:, LANES - 1:LANES]


def _cumulative_logf(lft):
    b, rows, t_len = lft.shape
    spec = pl.BlockSpec((1, rows, t_len), lambda i: (i, 0, 0))
    return pl.pallas_call(
        functools.partial(_cumsum_kernel, t_len=t_len),
        grid=(b,), in_specs=[spec], out_specs=spec,
        out_shape=jax.ShapeDtypeStruct((b, rows, t_len), F32),
        compiler_params=pltpu.CompilerParams(
            dimension_semantics=("arbitrary",), vmem_limit_bytes=VMEM_LIMIT),
        name="cumsum_logf",
    )(lft)


def _key_bias_layout(c, n_heads, tk):
    b, _, t_k = c.shape
    c = c[:, :n_heads].reshape(b, n_heads // HEADS_PER_TILE, HEADS_PER_TILE, t_k // tk, tk)
    c = jnp.transpose(c, (0, 1, 3, 2, 4))
    return jnp.pad(c, ((0, 0), (0, 0), (0, 0), (0, 8 - HEADS_PER_TILE), (0, 0)))


def _attn_kernel(*refs, mode, tq, tk, ks, past, cached):
    refs = list(refs)
    q_ref = refs.pop(0)
    if cached:
        ckt_ref, cvt_ref, nkt_ref, nvt_ref = refs[:4]
        refs = refs[4:]
    else:
        kt_ref, vt_ref = refs[:2]
        refs = refs[2:]
    c_ref = refs.pop(0) if mode == "fox" else None
    (o_ref,) = refs

    spb = ks // tk
    if cached:
        n_super = past // ks
        diag_blocks = [past // tk]

        def kv_block(j):
            if j == past // tk:
                return nkt_ref[0], nvt_ref[0]
            cols = slice(j * tk, (j + 1) * tk)
            return ckt_ref[0, 0, :, cols].astype(BF16), cvt_ref[0, 0, :, cols].astype(BF16)
    else:
        n_super = pl.program_id(2) * (tq // ks)
        diag_blocks = [n_super * spb + t for t in range(tq // tk)]

        def kv_block(j):
            return kt_ref[0, j], vt_ref[0, j]

    q = q_ref[0]
    qh = [jnp.where(_head_mask(tq, hh), q, jnp.zeros_like(q))
          for hh in range(HEADS_PER_TILE)]

    def visible_mask(width, koff, strict):
        r = lax.broadcasted_iota(jnp.int32, (tq, width), 0)
        c = lax.broadcasted_iota(jnp.int32, (tq, width), 1) + koff
        return c < r if strict else c <= r

    if mode == "fox":
        def scores(blocks, masked, hh):
            s = [jnp.dot(qh[hh], kv_block(j)[0], preferred_element_type=F32) for j in blocks]
            cb = [c_ref[0, 0, j][hh:hh + 1, :] for j in blocks]
            s = (s[0] if len(s) == 1 else jnp.concatenate(s, axis=1))
            s = s - (cb[0] if len(cb) == 1 else jnp.concatenate(cb, axis=1))
            if masked:
                s = jnp.where(visible_mask(len(blocks) * tk, 0, strict=False), s, NEG)
            return s

        def softmax_step(s, m, l):
            m_new = jnp.maximum(m, jnp.max(s, axis=1, keepdims=True))
            alpha = jnp.exp2(m - m_new)
            p = jnp.exp2(s - m_new)
            lanes = p[:, :LANES]
            for t in range(1, s.shape[1] // LANES):
                lanes = lanes + p[:, t * LANES:(t + 1) * LANES]
            return m_new, alpha * l + lanes, alpha, p.astype(BF16)

        def weighted_values(acc, alpha, p, blocks):
            acc = alpha * acc
            for t, j in enumerate(blocks):
                acc = acc + lax.dot_general(p[:, t * tk:(t + 1) * tk], kv_block(j)[1], NT_DIMS,
                                            preferred_element_type=F32)
            return acc

        heads = range(HEADS_PER_TILE)
        m = [jnp.full((tq, 1), NEG, F32) for _ in heads]
        l = [jnp.zeros((tq, LANES), F32) for _ in heads]
        acc = [jnp.zeros((tq, LANES), F32) for _ in heads]
        def step(carry, blocks, masked):
            m, l, acc = (list(carry[i::3]) for i in range(3))
            for hh in heads:
                m[hh], l[hh], alpha, p = softmax_step(scores(blocks, masked, hh), m[hh], l[hh])
                acc[hh] = weighted_values(acc[hh], alpha, p, blocks)
            return tuple(x for hh in heads for x in (m[hh], l[hh], acc[hh]))

        carry = step(tuple(x for hh in heads for x in (m[hh], l[hh], acc[hh])), diag_blocks, True)
        if cached:
            if n_super:
                carry = step(carry, list(range(n_super * spb)), False)
        else:
            carry = lax.fori_loop(
                0, n_super,
                lambda j, cy: step(cy, [j * spb + t for t in range(spb)], False), carry)
        m, l, acc = (list(carry[i::3]) for i in range(3))
        head_out = [acc[hh] / jnp.sum(l[hh], axis=1, keepdims=True) for hh in heads]
    else:
        r = lax.broadcasted_iota(jnp.int32, (tk, tk), 0)
        c = lax.broadcasted_iota(jnp.int32, (tk, tk), 1)
        later = (r > c).astype(BF16)

        def update(state, blocks, masked):
            kv = [kv_block(j) for j in blocks]
            if masked:
                visible = [visible_mask(tk, t * tk, strict=True) for t in range(len(blocks))]
            out = []
            for hh in range(HEADS_PER_TILE):
                run, acc = state[2 * hh:2 * hh + 2]
                logits, block_sums = [], []
                for t, (kt, _) in enumerate(kv):
                    z = jnp.dot(qh[hh], kt, preferred_element_type=F32)
                    sp = jnp.maximum(z, 0.0) + jnp.log2(1.0 + jnp.exp2(-jnp.abs(z)))
                    if masked:
                        sp = jnp.where(visible[t], sp, 0.0)
                    rest = jnp.dot(sp.astype(BF16), later, preferred_element_type=F32)
                    logits.append(z - sp - rest)
                    block_sums.append(jnp.sum(sp, axis=1, keepdims=True))
                for t in reversed(range(len(kv))):
                    a = jnp.exp2(logits[t] - run)
                    if masked:
                        a = jnp.where(visible[t], a, 0.0)
                    acc = acc + lax.dot_general(a.astype(BF16), kv[t][1], NT_DIMS,
                                                preferred_element_type=F32)
                    run = run + block_sums[t]
                out += [run, acc]
            return tuple(out)

        def weights_vanish(state):
            run_min = jnp.minimum(jnp.min(state[0]), jnp.min(state[2]))
            return run_min > DEAD_LOG2

        super_blocks = lambda sb: [sb * spb + b for b in range(spb)]
        state = (jnp.zeros((tq, 1), F32), jnp.zeros((tq, LANES), F32)) * HEADS_PER_TILE
        state = update(state, diag_blocks, True)
        if cached:
            for sb in reversed(range(n_super)):
                state = lax.cond(weights_vanish(state), lambda st: st,
                                 lambda st, sb=sb: update(st, super_blocks(sb), False), state)
        else:
            _, state = lax.while_loop(
                lambda cy: (cy[0] < n_super) & jnp.logical_not(weights_vanish(cy[1])),
                lambda cy: (cy[0] + 1, update(cy[1], super_blocks(n_super - 1 - cy[0]), False)),
                (jnp.int32(0), state))
        head_out = [state[2 * hh + 1] for hh in range(HEADS_PER_TILE)]

    o_ref[0] = jnp.where(_head_mask(tq, 0), head_out[0], head_out[1]).astype(o_ref.dtype)


def _attention(q, kv, c, *, mode, tq, tk, ks, past=0, cache_slot=0):
    b, t_q, width = q.shape
    cached = len(kv) == 4
    pairs = width // LANES
    q_spec = pl.BlockSpec((1, tq, LANES), lambda bi, hp, qi: (bi, qi, hp))
    if cached:
        assert t_q == tq and tq <= tk and past % ks == 0 and ks % tk == 0
        old = pl.BlockSpec((1, 1, LANES, past), lambda bi, hp, qi: (cache_slot, bi, hp, 0))
        new = pl.BlockSpec((1, LANES, tk), lambda bi, hp, qi: (bi, hp, 0))
        kv_specs = [old, old, new, new]
    else:
        assert t_q % tq == 0 and tq % ks == 0 and ks % tk == 0 and past == 0
        blocked = pl.BlockSpec((1, t_q // tk, LANES, tk), lambda bi, hp, qi: (bi, 0, hp, 0))
        kv_specs = [blocked, blocked]
    in_specs = [q_spec] + kv_specs
    args = [q, *kv]
    if mode == "fox":
        in_specs.append(pl.BlockSpec((1, 1) + c.shape[2:], lambda bi, hp, qi: (bi, hp, 0, 0, 0)))
        args.append(c)
    return pl.pallas_call(
        functools.partial(_attn_kernel, mode=mode, tq=tq, tk=tk, ks=ks, past=past, cached=cached),
        grid=(b, pairs, t_q // tq),
        in_specs=in_specs,
        out_specs=q_spec,
        out_shape=jax.ShapeDtypeStruct((b, t_q, width), BF16),
        compiler_params=pltpu.CompilerParams(
            dimension_semantics=("arbitrary", "arbitrary", "arbitrary"),
            vmem_limit_bytes=VMEM_LIMIT),
        name=f"attn_{mode}",
    )(*args)


def _post_kernel(*refs, nb, mix, final):
    if final:
        (x_ref, o_ref, qm_ref, mkt_ref, mvt_ref, wo_ref, g_ref, wgu_ref, wd_ref, gf_ref,
         y_ref, acc_ref) = refs
    else:
        (x_ref, o_ref, qm_ref, mkt_ref, mvt_ref, wo_ref, g_ref, wgu_ref, wd_ref,
         xo_ref, acc_ref) = refs
    tm = x_ref.shape[0]
    rb = tm // nb
    memw = qm_ref.shape[1]

    om_rows = []
    for bi in range(nb):
        tiles = []
        for tl in range(memw // LANES):
            feats = slice(tl * LANES, (tl + 1) * LANES)
            qp = qm_ref[bi * rb:(bi + 1) * rb, feats]
            mkt = mkt_ref[0, bi, feats, :].astype(BF16)
            mvt = mvt_ref[0, bi, feats, :].astype(BF16)
            tile = None
            for hh in range(HEADS_PER_TILE):
                in_head = _head_mask(rb, hh)
                qh = jnp.where(in_head, qp, jnp.zeros_like(qp))
                s = jnp.dot(qh, mkt, preferred_element_type=F32)
                p = jnp.exp2(s - jnp.max(s, axis=1, keepdims=True))
                o = lax.dot_general(p.astype(BF16), mvt, NT_DIMS, preferred_element_type=F32)
                o = o / jnp.sum(p, axis=1, keepdims=True)
                tile = o if tile is None else jnp.where(in_head, o, tile)
            tiles.append(tile)
        om_rows.append(jnp.concatenate(tiles, axis=1))
    om = (om_rows[0] if nb == 1 else jnp.concatenate(om_rows, axis=0)).astype(BF16)

    x1 = (x_ref[...]
          + jnp.dot(o_ref[...], wo_ref[0:mix, :], preferred_element_type=F32)
          + jnp.dot(om, wo_ref[mix:mix + memw, :], preferred_element_type=F32))
    h = (_rms_scale(x1) * g_ref[...]).astype(BF16)
    ffc = wd_ref.shape[1]

    acc_ref[...] = x1

    def chunk(ci, _):
        gu = jnp.dot(h, wgu_ref[ci], preferred_element_type=F32)
        gate = gu[:, :ffc]
        act = (gate * jax.nn.sigmoid(gate) * gu[:, ffc:]).astype(BF16)
        acc_ref[...] += jnp.dot(act, wd_ref[ci], preferred_element_type=F32)
        return 0

    lax.fori_loop(0, wgu_ref.shape[0], chunk, 0)
    if final:
        y_ref[...] = _rms_scale(acc_ref[...]) * gf_ref[...]
    else:
        xo_ref[...] = acc_ref[...]


def _post(x, o_mix, qm, mkt, mvt, layer, w_out, g_ffn, w_gu, w_down, g_final, *, tm, nb):
    rows, d = x.shape
    mix = o_mix.shape[1]
    memw = qm.shape[1]
    n_mem = mkt.shape[3]
    final = g_final is not None
    rows_per_batch = rows // mkt.shape[1]
    assert tm == nb * rows_per_batch or (nb == 1 and rows_per_batch % tm == 0)
    blocks_per_batch = max(rows_per_batch // tm, 1)
    row = lambda width: pl.BlockSpec((tm, width), lambda r: (r, 0))
    mem_spec = pl.BlockSpec((1, nb, memw, n_mem), lambda r: (layer, r // blocks_per_batch, 0, 0))
    in_specs = [row(d), row(mix), row(memw), mem_spec, mem_spec,
                _const_spec(w_out.shape), _const_spec((1, d)),
                _const_spec(w_gu.shape), _const_spec(w_down.shape)]
    args = [x, o_mix, qm, mkt, mvt, w_out, g_ffn.reshape(1, d), w_gu, w_down]
    if final:
        in_specs.append(_const_spec((1, d)))
        args.append(g_final.reshape(1, d))
    return pl.pallas_call(
        functools.partial(_post_kernel, nb=nb, mix=mix, final=final),
        grid=(rows // tm,),
        in_specs=in_specs,
        out_specs=row(d),
        out_shape=jax.ShapeDtypeStruct((rows, d), F32),
        scratch_shapes=[pltpu.VMEM((tm, d), F32)],
        compiler_params=pltpu.CompilerParams(
            dimension_semantics=("arbitrary",), vmem_limit_bytes=VMEM_LIMIT),
        name="post_final" if final else "post",
    )(*args)


def _prep_w_in(w, mix, n_forget):
    wn = jnp.concatenate([w[:, :mix], w[:, 3 * mix + n_forget:]], axis=1)
    wt = w[:, mix:3 * mix + n_forget]
    if n_forget:
        wt = jnp.pad(wt, ((0, 0), (0, FORGET_ROWS - n_forget)))
    return wn.astype(BF16), wt.T.astype(BF16)


def _ff_chunk(d_ff):
    for c in (1408, 512, 256, 128):
        if d_ff % c == 0:
            return c
    raise ValueError(f"d_ff={d_ff} is not a multiple of {LANES}")


def _prep_ffn(w_gate_up, w_down):
    d, two_ff = w_gate_up.shape
    d_ff = two_ff // 2
    ffc = _ff_chunk(d_ff)
    n = d_ff // ffc
    gate = w_gate_up[:, :d_ff].reshape(d, n, ffc)
    up = w_gate_up[:, d_ff:].reshape(d, n, ffc)
    w_gu = jnp.transpose(jnp.concatenate([gate, up], axis=2), (1, 0, 2)).astype(BF16)
    return w_gu, w_down.reshape(n, ffc, d).astype(BF16)


def _feature_major(cache):
    *lead, t, h, hd = cache.shape
    return jnp.moveaxis(cache, -3, -1).reshape(*lead, h * hd, t)


def _token_major(xt):
    *lead, width, t = xt.shape
    return jnp.moveaxis(xt.reshape(*lead, width // HEAD_DIM, HEAD_DIM, t), -1, -3)


def kernel(x_prompt, x_sample, mem_prompt, cache_fox_k, cache_fox_v, cache_fox_logf, cache_sb_k, cache_sb_v, cache_mem_k, cache_mem_v, g_mix, w_in_fox, b_f, w_in_sb, g_mem, w_mem_kv, w_out, g_ffn, w_gate_up, w_down, g_final):
    bp, seq, d = x_prompt.shape
    bs, dec_seq, _ = x_sample.shape
    n_mem = mem_prompt.shape[1]
    depth = g_mix.shape[0]
    n_heads = b_f.shape[1]
    mix = n_heads * HEAD_DIM
    memw = w_mem_kv.shape[2] // 2
    past = cache_fox_k.shape[2]

    tk = 256
    ks = 512
    tq_p = ks
    tm_p = 512
    rows_s = bs * dec_seq
    assert seq % tq_p == 0 and seq % tm_p == 0 and rows_s % tk == 0
    assert dec_seq <= tk and past % ks == 0

    xp = x_prompt.reshape(bp * seq, d)
    xs = x_sample.reshape(rows_s, d)

    mkt_p, mvt_p = _memory_kv(mem_prompt, g_mem, jnp.transpose(w_mem_kv, (0, 2, 1)).astype(BF16))

    def new_keys(xt):
        xt = jnp.transpose(xt.reshape(mix, bs, dec_seq), (1, 0, 2))
        return jnp.pad(xt, ((0, 0), (0, 0), (0, tk - dec_seq))).astype(BF16)

    caches = {"fox": (_feature_major(cache_fox_k), _feature_major(cache_fox_v)),
              "sb": (_feature_major(cache_sb_k), _feature_major(cache_sb_v))}
    cmkt, cmvt = _feature_major(cache_mem_k), _feature_major(cache_mem_v)
    n_layers = {"fox": (depth + 1) // 2, "sb": depth // 2}
    kv_p = {"fox": None, "sb": None}
    fl_p = []
    fk_s, fv_s, fl_s, sk_s, sv_s = [], [], [], [], []
    for i in range(depth):
        j = i // 2
        is_fox = i % 2 == 0
        mode = "fox" if is_fox else "sb"
        if is_fox:
            wn, wt = _prep_w_in(w_in_fox[j], mix, n_heads)
            bias = jnp.pad(b_f[j], (0, FORGET_ROWS - n_heads)).reshape(FORGET_ROWS, 1)
        else:
            wn, wt = _prep_w_in(w_in_sb[j], mix, 0)
            bias = None
        w_gu, w_dn = _prep_ffn(w_gate_up[i], w_down[i])
        w_o = w_out[i].astype(BF16)
        g_fin = g_final if i == depth - 1 else None

        outs = _project(xp, g_mix[i], wn, wt, bias, batch=bp, mix=mix, memw=memw, tm=tm_p, tk=tk,
                        slot=j, n_slots=n_layers[mode], stacked=kv_p[mode])
        q, qm, kt, vt, ktb, vtb = outs[:6]
        kv_p[mode] = (kt, vt)
        c = None
        if is_fox:
            lft = outs[6]
            c = _key_bias_layout(_cumulative_logf(lft), n_heads, tk)
            fl_p.append(jnp.transpose(lft[:, :n_heads], (0, 2, 1)))
        o = _attention(q.reshape(bp, seq, mix), (ktb, vtb), c, mode=mode, tq=tq_p, tk=tk, ks=ks)
        xp_new = _post(xp, o.reshape(bp * seq, mix), qm, mkt_p, mvt_p, i,
                       w_o, g_ffn[i], w_gu, w_dn, g_fin, tm=tm_p, nb=1)

        outs = _project(xs, g_mix[i], wn, wt, bias, batch=1, mix=mix, memw=memw, tm=rows_s, tk=tk)
        q, qm, kt, vt = outs[:4]
        c = None
        if is_fox:
            lft = outs[6]
            lft_s = jnp.transpose(lft.reshape(FORGET_ROWS, bs, dec_seq), (1, 0, 2))
            lft_past = jnp.pad(jnp.transpose(cache_fox_logf[j], (0, 2, 1)),
                               ((0, 0), (0, FORGET_ROWS - n_heads), (0, 0)))
            lft_all = jnp.concatenate(
                [lft_past, jnp.pad(lft_s, ((0, 0), (0, 0), (0, tk - dec_seq)))], axis=2)
            c = _key_bias_layout(_cumulative_logf(lft_all), n_heads, tk)
            fl_s.append(jnp.transpose(lft_s[:, :n_heads], (0, 2, 1)))
        new_shape = (n_heads, HEAD_DIM, bs, dec_seq)
        k_new = jnp.transpose(kt.reshape(new_shape), (2, 3, 0, 1))
        v_new = jnp.transpose(vt.reshape(new_shape), (2, 3, 0, 1))
        if is_fox:
            fk_s.append(k_new)
            fv_s.append(v_new)
        else:
            sk_s.append(k_new)
            sv_s.append(v_new)
        o = _attention(q.reshape(bs, dec_seq, mix),
                       (*caches[mode], new_keys(kt), new_keys(vt)), c,
                       mode=mode, tq=dec_seq, tk=tk, ks=ks, past=past, cache_slot=j)
        xs_new = _post(xs, o.reshape(rows_s, mix), qm, cmkt, cmvt, i,
                       w_o, g_ffn[i], w_gu, w_dn, g_fin, tm=rows_s, nb=bs)
        xp, xs = xp_new, xs_new

    return (xp.reshape(bp, seq, d), xs.reshape(bs, dec_seq, d),
            _token_major(kv_p["fox"][0]), _token_major(kv_p["fox"][1]), jnp.stack(fl_p),
            _token_major(kv_p["sb"][0]), _token_major(kv_p["sb"][1]),
            _token_major(mkt_p), _token_major(mvt_p),
            jnp.stack(fk_s), jnp.stack(fv_s), jnp.stack(fl_s), jnp.stack(sk_s), jnp.stack(sv_s))
```

```python
import functools

import jax
import jax.numpy as jnp
from jax import lax
from jax.experimental import pallas as pl
from jax.experimental.pallas import tpu as pltpu

HEAD_DIM = 64
LANES = 128
HEADS_PER_TILE = LANES // HEAD_DIM
FORGET_ROWS = 16
EPS = 1e-6
LOG2E = 1.4426950408889634
QK_SCALE = HEAD_DIM ** -0.5 * LOG2E
NEG = -1e30
DEAD_LOG2 = 160.0
VMEM_LIMIT = 56 * 1024 * 1024

F32 = jnp.float32
BF16 = jnp.bfloat16
NT_DIMS = (((1,), (1,)), ((), ()))


def _rms_scale(x):
    return x * lax.rsqrt(jnp.mean(x * x, axis=-1, keepdims=True) + EPS)


def _const_spec(shape):
    return pl.BlockSpec(shape, lambda *_: (0,) * len(shape), pipeline_mode=pl.Buffered(1))


def _head_mask(rows, hh):
    lane = lax.broadcasted_iota(jnp.int32, (rows, LANES), 1)
    return (lane >= hh * HEAD_DIM) & (lane < (hh + 1) * HEAD_DIM)


def _proj_kernel(*refs, mix, tk, has_forget, n_aliased):
    refs = list(refs)
    x_ref, g_ref, wn_ref, wt_ref = refs[:4]
    del refs[:4]
    if has_forget:
        bf_ref = refs.pop(0)
    del refs[:n_aliased]
    q_ref, qm_ref, kt_ref, vt_ref, ktb_ref, vtb_ref = refs[:6]
    if has_forget:
        lft_ref = refs[6]
    tm = x_ref.shape[0]
    hn = (_rms_scale(x_ref[...]) * g_ref[...]).astype(BF16)
    qq = jnp.dot(hn, wn_ref[...], preferred_element_type=F32) * QK_SCALE
    q_ref[...] = qq[:, :mix].astype(BF16)
    qm_ref[...] = qq[:, mix:].astype(BF16)
    t = lax.dot_general(wt_ref[...], hn, NT_DIMS, preferred_element_type=F32)
    kt = t[:mix]
    vt = t[mix:2 * mix]
    kt_ref[0, 0] = kt
    vt_ref[0, 0] = vt
    for c in range(tm // tk):
        ktb_ref[0, c] = kt[:, c * tk:(c + 1) * tk].astype(BF16)
        vtb_ref[0, c] = vt[:, c * tk:(c + 1) * tk].astype(BF16)
    if has_forget:
        f = t[2 * mix:] + bf_ref[...]
        lft_ref[0] = jnp.minimum(f, 0.0) - jnp.log(1.0 + jnp.exp(-jnp.abs(f)))


def _project(x, g, wn, wt, bias, *, batch, mix, memw, tm, tk, slot=0, n_slots=1, stacked=None):
    rows, d = x.shape
    t_len = rows // batch
    per_batch = t_len // tm
    has_forget = bias is not None
    row = lambda width: pl.BlockSpec((tm, width), lambda r: (r, 0))
    feat = pl.BlockSpec((1, FORGET_ROWS, tm), lambda r: (r // per_batch, 0, r % per_batch))
    slotted = pl.BlockSpec((1, 1, mix, tm), lambda r: (slot, r // per_batch, 0, r % per_batch))
    blocked = pl.BlockSpec((1, tm // tk, mix, tk), lambda r: (r // per_batch, r % per_batch, 0, 0))
    in_specs = [row(d), _const_spec((1, d)), _const_spec(wn.shape), _const_spec(wt.shape)]
    args = [x, g.reshape(1, d), wn, wt]
    if has_forget:
        in_specs.append(_const_spec((FORGET_ROWS, 1)))
        args.append(bias)
    aliases = {}
    if stacked is not None:
        for a in stacked:
            aliases[len(args)] = 2 + len(aliases)
            in_specs.append(pl.BlockSpec(memory_space=pl.ANY))
            args.append(a)
    out_shape = [jax.ShapeDtypeStruct((rows, mix), BF16),
                 jax.ShapeDtypeStruct((rows, memw), BF16),
                 jax.ShapeDtypeStruct((n_slots, batch, mix, t_len), F32),
                 jax.ShapeDtypeStruct((n_slots, batch, mix, t_len), F32),
                 jax.ShapeDtypeStruct((batch, t_len // tk, mix, tk), BF16),
                 jax.ShapeDtypeStruct((batch, t_len // tk, mix, tk), BF16)]
    out_specs = [row(mix), row(memw), slotted, slotted, blocked, blocked]
    if has_forget:
        out_shape.append(jax.ShapeDtypeStruct((batch, FORGET_ROWS, t_len), F32))
        out_specs.append(feat)
    return pl.pallas_call(
        functools.partial(_proj_kernel, mix=mix, tk=tk, has_forget=has_forget,
                          n_aliased=len(aliases)),
        grid=(rows // tm,),
        in_specs=in_specs, out_specs=out_specs, out_shape=out_shape,
        input_output_aliases=aliases,
        compiler_params=pltpu.CompilerParams(
            dimension_semantics=("arbitrary",), vmem_limit_bytes=VMEM_LIMIT),
        name="proj_fox" if has_forget else "proj_sb",
    )(*args)


def _memkv_kernel(x_ref, g_ref, wt_ref, kt_ref, vt_ref, *, memw):
    hn = (_rms_scale(x_ref[0]) * g_ref[0]).astype(BF16)
    kvt = lax.dot_general(wt_ref[0], hn, NT_DIMS, preferred_element_type=F32)
    kt_ref[0, 0] = kvt[:memw]
    vt_ref[0, 0] = kvt[memw:]


def _memory_kv(mem, g_mem, wt_mem_kv):
    batch, n_mem, d = mem.shape
    depth, two_memw, _ = wt_mem_kv.shape
    memw = two_memw // 2
    out_spec = pl.BlockSpec((1, 1, memw, n_mem), lambda l, b: (l, b, 0, 0))
    return pl.pallas_call(
        functools.partial(_memkv_kernel, memw=memw),
        grid=(depth, batch),
        in_specs=[pl.BlockSpec((1, n_mem, d), lambda l, b: (b, 0, 0)),
                  pl.BlockSpec((1, 1, d), lambda l, b: (l, 0, 0)),
                  pl.BlockSpec((1, two_memw, d), lambda l, b: (l, 0, 0))],
        out_specs=[out_spec] * 2,
        out_shape=[jax.ShapeDtypeStruct((depth, batch, memw, n_mem), F32)] * 2,
        compiler_params=pltpu.CompilerParams(
            dimension_semantics=("arbitrary", "arbitrary"), vmem_limit_bytes=VMEM_LIMIT),
        name="memkv",
    )(mem, g_mem.reshape(depth, 1, d), wt_mem_kv)


def _split3(x):
    hi = x.astype(BF16)
    r1 = x - hi.astype(F32)
    mid = r1.astype(BF16)
    lo = (r1 - mid.astype(F32)).astype(BF16)
    return hi, mid, lo


def _cumsum_kernel(lf_ref, c_ref, *, t_len):
    r = lax.broadcasted_iota(jnp.int32, (LANES, LANES), 0)
    c = lax.broadcasted_iota(jnp.int32, (LANES, LANES), 1)
    upper = (r <= c).astype(BF16)
    carry = jnp.zeros((lf_ref.shape[1], 1), F32)
    for ch in range(t_len // LANES):
        cs = carry
        for part in _split3(lf_ref[0, :, ch * LANES:(ch + 1) * LANES]):
            cs = cs + jnp.dot(part, upper, preferred_element_type=F32)
        c_ref[0, :, ch * LANES:(ch + 1) * LANES] = cs * LOG2E
        carry = cs[:, LANES - 1:LANES]


def _cumulative_logf(lft):
    b, rows, t_len = lft.shape
    spec = pl.BlockSpec((1, rows, t_len), lambda i: (i, 0, 0))
    return pl.pallas_call(
        functools.partial(_cumsum_kernel, t_len=t_len),
        grid=(b,), in_specs=[spec], out_specs=spec,
        out_shape=jax.ShapeDtypeStruct((b, rows, t_len), F32),
        compiler_params=pltpu.CompilerParams(
            dimension_semantics=("arbitrary",), vmem_limit_bytes=VMEM_LIMIT),
        name="cumsum_logf",
    )(lft)


def _key_bias_layout(c, n_heads, tk):
    b, _, t_k = c.shape
    c = c[:, :n_heads].reshape(b, n_heads // HEADS_PER_TILE, HEADS_PER_TILE, t_k // tk, tk)
    c = jnp.transpose(c, (0, 1, 3, 2, 4))
    return jnp.pad(c, ((0, 0), (0, 0), (0, 0), (0, 8 - HEADS_PER_TILE), (0, 0)))


def _attn_kernel(*refs, mode, tq, tk, ks, past, cached):
    refs = list(refs)
    q_ref = refs.pop(0)
    if cached:
        ckt_ref, cvt_ref, nkt_ref, nvt_ref = refs[:4]
        refs = refs[4:]
    else:
        kt_ref, vt_ref = refs[:2]
        refs = refs[2:]
    c_ref = refs.pop(0) if mode == "fox" else None
    (o_ref,) = refs

    spb = ks // tk
    if cached:
        n_super = past // ks
        diag_blocks = [past // tk]

        def kv_block(j):
            if j == past // tk:
                return nkt_ref[0], nvt_ref[0]
            cols = slice(j * tk, (j + 1) * tk)
            return ckt_ref[0, 0, :, cols].astype(BF16), cvt_ref[0, 0, :, cols].astype(BF16)
    else:
        n_super = pl.program_id(2) * (tq // ks)
        diag_blocks = [n_super * spb + t for t in range(tq // tk)]

        def kv_block(j):
            return kt_ref[0, j], vt_ref[0, j]

    q = q_ref[0]
    qh = [jnp.where(_head_mask(tq, hh), q, jnp.zeros_like(q))
          for hh in range(HEADS_PER_TILE)]

    heads = range(HEADS_PER_TILE)
    groups = [(slice(d * tk, min((d + 1) * tk, tq)), diag_blocks[:d + 1], d * tk)
              for d in range(len(diag_blocks))]

    def take(x, rows):
        return x if rows == slice(None) else x[rows]

    def visible_mask(rows, width, koff, roff, strict):
        r = lax.broadcasted_iota(jnp.int32, (rows, width), 0) + roff
        c = lax.broadcasted_iota(jnp.int32, (rows, width), 1) + koff
        return c < r if strict else c <= r

    if mode == "fox":
        def scores(blocks, hh, rows=slice(None), roff=None):
            qr = take(qh[hh], rows)
            s = [jnp.dot(qr, kv_block(j)[0], preferred_element_type=F32) for j in blocks]
            cb = [c_ref[0, 0, j][hh:hh + 1, :] for j in blocks]
            s = (s[0] if len(s) == 1 else jnp.concatenate(s, axis=1))
            s = s - (cb[0] if len(cb) == 1 else jnp.concatenate(cb, axis=1))
            if roff is not None:
                s = jnp.where(visible_mask(s.shape[0], s.shape[1], 0, roff, strict=False), s, NEG)
            return s

        def softmax_step(s, m, l):
            m_new = jnp.maximum(m, jnp.max(s, axis=1, keepdims=True))
            alpha = jnp.exp2(m - m_new)
            p = jnp.exp2(s - m_new)
            lanes = p[:, :LANES]
            for t in range(1, s.shape[1] // LANES):
                lanes = lanes + p[:, t * LANES:(t + 1) * LANES]
            return m_new, alpha * l + lanes, alpha, p.astype(BF16)

        def weighted_values(acc, alpha, p, blocks):
            acc = alpha * acc
            for t, j in enumerate(blocks):
                acc = acc + lax.dot_general(p[:, t * tk:(t + 1) * tk], kv_block(j)[1], NT_DIMS,
                                            preferred_element_type=F32)
            return acc

        def step(carry, specs):
            m, l, acc = (list(carry[i::3]) for i in range(3))
            s = [[scores(blocks, hh, rows, roff) for hh in heads] for rows, blocks, roff in specs]
            new = [[None] * len(specs) for _ in heads]
            for g, (rows, blocks, _) in enumerate(specs):
                for hh in heads:
                    new[hh][g] = softmax_step(s[g][hh], take(m[hh], rows), take(l[hh], rows))
            out = []
            for hh in heads:
                parts = []
                for g, (rows, blocks, _) in enumerate(specs):
                    m_g, l_g, alpha, p = new[hh][g]
                    parts.append((m_g, l_g, weighted_values(take(acc[hh], rows), alpha, p, blocks)))
                out += [parts[0][i] if len(parts) == 1 else jnp.concatenate([x[i] for x in parts], axis=0)
                        for i in range(3)]
            return tuple(out)

        carry = (jnp.full((tq, 1), NEG, F32), jnp.zeros((tq, LANES), F32),
                 jnp.zeros((tq, LANES), F32)) * HEADS_PER_TILE
        whole = lambda blocks: [(slice(None), blocks, None)]
        if cached:
            if n_super:
                carry = step(carry, whole(list(range(n_super * spb))))
        else:
            carry = lax.fori_loop(
                0, n_super,
                lambda j, cy: step(cy, whole([j * spb + t for t in range(spb)])), carry)
        carry = step(carry, groups)
        head_out = [carry[3 * hh + 2] / jnp.sum(carry[3 * hh + 1], axis=1, keepdims=True)
                    for hh in heads]
    else:
        r = lax.broadcasted_iota(jnp.int32, (tk, tk), 0)
        c = lax.broadcasted_iota(jnp.int32, (tk, tk), 1)
        later = (r > c).astype(BF16)

        def update(state, blocks, rows=slice(None), roff=None):
            kv = [kv_block(j) for j in blocks]
            nrows = take(state[0], rows).shape[0]
            visible = [None] * len(kv)
            if roff is not None:
                visible = [None if (t + 1) * tk <= roff
                           else visible_mask(nrows, tk, t * tk, roff, strict=True)
                           for t in range(len(kv))]
            out = []
            for hh in heads:
                run, acc = take(state[2 * hh], rows), take(state[2 * hh + 1], rows)
                qr = take(qh[hh], rows)
                logits, block_sums = [], []
                for t, (kt, _) in enumerate(kv):
                    z = jnp.dot(qr, kt, preferred_element_type=F32)
                    sp = jnp.maximum(z, 0.0) + jnp.log2(1.0 + jnp.exp2(-jnp.abs(z)))
                    if visible[t] is not None:
                        sp = jnp.where(visible[t], sp, 0.0)
                    rest = jnp.dot(sp.astype(BF16), later, preferred_element_type=F32)
                    logits.append(z - sp - rest)
                    block_sums.append(jnp.sum(sp, axis=1, keepdims=True))
                for t in reversed(range(len(kv))):
                    a = jnp.exp2(logits[t] - run)
                    if visible[t] is not None:
                        a = jnp.where(visible[t], a, 0.0)
                    acc = acc + lax.dot_general(a.astype(BF16), kv[t][1], NT_DIMS,
                                                preferred_element_type=F32)
                    run = run + block_sums[t]
                out += [run, acc]
            return tuple(out)

        def weights_vanish(state):
            run_min = jnp.minimum(jnp.min(state[0]), jnp.min(state[2]))
            return run_min > DEAD_LOG2

        super_blocks = lambda sb: [sb * spb + b for b in range(spb)]
        state = (jnp.zeros((tq, 1), F32), jnp.zeros((tq, LANES), F32)) * HEADS_PER_TILE
        state = update(state, diag_blocks, roff=0)
        if cached:
            for sb in reversed(range(n_super)):
                state = lax.cond(weights_vanish(state), lambda st: st,
                                 lambda st, sb=sb: update(st, super_blocks(sb)), state)
        else:
            _, state = lax.while_loop(
                lambda cy: (cy[0] < n_super) & jnp.logical_not(weights_vanish(cy[1])),
                lambda cy: (cy[0] + 1, update(cy[1], super_blocks(n_super - 1 - cy[0]))),
                (jnp.int32(0), state))
        head_out = [state[2 * hh + 1] for hh in heads]

    o_ref[0] = jnp.where(_head_mask(tq, 0), head_out[0], head_out[1]).astype(o_ref.dtype)


def _attention(q, kv, c, *, mode, tq, tk, ks, past=0, cache_slot=0):
    b, t_q, width = q.shape
    cached = len(kv) == 4
    pairs = width // LANES
    q_spec = pl.BlockSpec((1, tq, LANES), lambda bi, hp, qi: (bi, qi, hp))
    if cached:
        assert t_q == tq and tq <= tk and past % ks == 0 and ks % tk == 0
        old = pl.BlockSpec((1, 1, LANES, past), lambda bi, hp, qi: (cache_slot, bi, hp, 0))
        new = pl.BlockSpec((1, LANES, tk), lambda bi, hp, qi: (bi, hp, 0))
        kv_specs = [old, old, new, new]
    else:
        assert t_q % tq == 0 and tq % ks == 0 and ks % tk == 0 and past == 0
        blocked = pl.BlockSpec((1, t_q // tk, LANES, tk), lambda bi, hp, qi: (bi, 0, hp, 0))
        kv_specs = [blocked, blocked]
    in_specs = [q_spec] + kv_specs
    args = [q, *kv]
    if mode == "fox":
        in_specs.append(pl.BlockSpec((1, 1) + c.shape[2:], lambda bi, hp, qi: (bi, hp, 0, 0, 0)))
        args.append(c)
    return pl.pallas_call(
        functools.partial(_attn_kernel, mode=mode, tq=tq, tk=tk, ks=ks, past=past, cached=cached),
        grid=(b, pairs, t_q // tq),
        in_specs=in_specs,
        out_specs=q_spec,
        out_shape=jax.ShapeDtypeStruct((b, t_q, width), BF16),
        compiler_params=pltpu.CompilerParams(
            dimension_semantics=("arbitrary", "arbitrary", "arbitrary"),
            vmem_limit_bytes=VMEM_LIMIT),
        name=f"attn_{mode}",
    )(*args)


def _post_kernel(*refs, nb, mix, final):
    if final:
        (x_ref, o_ref, qm_ref, mkt_ref, mvt_ref, wo_ref, g_ref, wgu_ref, wd_ref, gf_ref,
         y_ref, acc_ref) = refs
    else:
        (x_ref, o_ref, qm_ref, mkt_ref, mvt_ref, wo_ref, g_ref, wgu_ref, wd_ref,
         xo_ref, acc_ref) = refs
    tm = x_ref.shape[0]
    rb = tm // nb
    memw = qm_ref.shape[1]

    om_rows = []
    for bi in range(nb):
        tiles = []
        for tl in range(memw // LANES):
            feats = slice(tl * LANES, (tl + 1) * LANES)
            qp = qm_ref[bi * rb:(bi + 1) * rb, feats]
            mkt = mkt_ref[0, bi, feats, :].astype(BF16)
            mvt = mvt_ref[0, bi, feats, :].astype(BF16)
            tile = None
            for hh in range(HEADS_PER_TILE):
                in_head = _head_mask(rb, hh)
                qh = jnp.where(in_head, qp, jnp.zeros_like(qp))
                s = jnp.dot(qh, mkt, preferred_element_type=F32)
                p = jnp.exp2(s - jnp.max(s, axis=1, keepdims=True))
                o = lax.dot_general(p.astype(BF16), mvt, NT_DIMS, preferred_element_type=F32)
                o = o / jnp.sum(p, axis=1, keepdims=True)
                tile = o if tile is None else jnp.where(in_head, o, tile)
            tiles.append(tile)
        om_rows.append(jnp.concatenate(tiles, axis=1))
    om = (om_rows[0] if nb == 1 else jnp.concatenate(om_rows, axis=0)).astype(BF16)

    x1 = (x_ref[...]
          + jnp.dot(o_ref[...], wo_ref[0:mix, :], preferred_element_type=F32)
          + jnp.dot(om, wo_ref[mix:mix + memw, :], preferred_element_type=F32))
    h = (_rms_scale(x1) * g_ref[...]).astype(BF16)
    ffc = wd_ref.shape[1]

    acc_ref[...] = x1

    def chunk(ci, _):
        gu = jnp.dot(h, wgu_ref[ci], preferred_element_type=F32)
        gate = gu[:, :ffc]
        act = (gate * jax.nn.sigmoid(gate) * gu[:, ffc:]).astype(BF16)
        acc_ref[...] += jnp.dot(act, wd_ref[ci], preferred_element_type=F32)
        return 0

    lax.fori_loop(0, wgu_ref.shape[0], chunk, 0)
    if final:
        y_ref[...] = _rms_scale(acc_ref[...]) * gf_ref[...]
    else:
        xo_ref[...] = acc_ref[...]


def _post(x, o_mix, qm, mkt, mvt, layer, w_out, g_ffn, w_gu, w_down, g_final, *, tm, nb):
    rows, d = x.shape
    mix = o_mix.shape[1]
    memw = qm.shape[1]
    n_mem = mkt.shape[3]
    final = g_final is not None
    rows_per_batch = rows // mkt.shape[1]
    assert tm == nb * rows_per_batch or (nb == 1 and rows_per_batch % tm == 0)
    blocks_per_batch = max(rows_per_batch // tm, 1)
    row = lambda width: pl.BlockSpec((tm, width), lambda r: (r, 0))
    mem_spec = pl.BlockSpec((1, nb, memw, n_mem), lambda r: (layer, r // blocks_per_batch, 0, 0))
    in_specs = [row(d), row(mix), row(memw), mem_spec, mem_spec,
                _const_spec(w_out.shape), _const_spec((1, d)),
                _const_spec(w_gu.shape), _const_spec(w_down.shape)]
    args = [x, o_mix, qm, mkt, mvt, w_out, g_ffn.reshape(1, d), w_gu, w_down]
    if final:
        in_specs.append(_const_spec((1, d)))
        args.append(g_final.reshape(1, d))
    return pl.pallas_call(
        functools.partial(_post_kernel, nb=nb, mix=mix, final=final),
        grid=(rows // tm,),
        in_specs=in_specs,
        out_specs=row(d),
        out_shape=jax.ShapeDtypeStruct((rows, d), F32),
        scratch_shapes=[pltpu.VMEM((tm, d), F32)],
        compiler_params=pltpu.CompilerParams(
            dimension_semantics=("arbitrary",), vmem_limit_bytes=VMEM_LIMIT),
        name="post_final" if final else "post",
    )(*args)


def _prep_w_in(w, mix, n_forget):
    wn = jnp.concatenate([w[:, :mix], w[:, 3 * mix + n_forget:]], axis=1)
    wt = w[:, mix:3 * mix + n_forget]
    if n_forget:
        wt = jnp.pad(wt, ((0, 0), (0, FORGET_ROWS - n_forget)))
    return wn.astype(BF16), wt.T.astype(BF16)


def _ff_chunk(d_ff):
    for c in (1408, 512, 256, 128):
        if d_ff % c == 0:
            return c
    raise ValueError(f"d_ff={d_ff} is not a multiple of {LANES}")


def _prep_ffn(w_gate_up, w_down):
    d, two_ff = w_gate_up.shape
    d_ff = two_ff // 2
    ffc = _ff_chunk(d_ff)
    n = d_ff // ffc
    gate = w_gate_up[:, :d_ff].reshape(d, n, ffc)
    up = w_gate_up[:, d_ff:].reshape(d, n, ffc)
    w_gu = jnp.transpose(jnp.concatenate([gate, up], axis=2), (1, 0, 2)).astype(BF16)
    return w_gu, w_down.reshape(n, ffc, d).astype(BF16)


def _feature_major(cache):
    *lead, t, h, hd = cache.shape
    return jnp.moveaxis(cache, -3, -1).reshape(*lead, h * hd, t)


def _token_major(xt):
    *lead, width, t = xt.shape
    return jnp.moveaxis(xt.reshape(*lead, width // HEAD_DIM, HEAD_DIM, t), -1, -3)


def kernel(x_prompt, x_sample, mem_prompt, cache_fox_k, cache_fox_v, cache_fox_logf, cache_sb_k, cache_sb_v, cache_mem_k, cache_mem_v, g_mix, w_in_fox, b_f, w_in_sb, g_mem, w_mem_kv, w_out, g_ffn, w_gate_up, w_down, g_final):
    bp, seq, d = x_prompt.shape
    bs, dec_seq, _ = x_sample.shape
    n_mem = mem_prompt.shape[1]
    depth = g_mix.shape[0]
    n_heads = b_f.shape[1]
    mix = n_heads * HEAD_DIM
    memw = w_mem_kv.shape[2] // 2
    past = cache_fox_k.shape[2]

    tk = 256
    ks = 512
    tq_p = ks
    tm_p = 512
    rows_s = bs * dec_seq
    assert seq % tq_p == 0 and seq % tm_p == 0 and rows_s % tk == 0
    assert dec_seq <= tk and past % ks == 0

    xp = x_prompt.reshape(bp * seq, d)
    xs = x_sample.reshape(rows_s, d)

    mkt_p, mvt_p = _memory_kv(mem_prompt, g_mem, jnp.transpose(w_mem_kv, (0, 2, 1)).astype(BF16))

    def new_keys(xt):
        xt = jnp.transpose(xt.reshape(mix, bs, dec_seq), (1, 0, 2))
        return jnp.pad(xt, ((0, 0), (0, 0), (0, tk - dec_seq))).astype(BF16)

    caches = {"fox": (_feature_major(cache_fox_k), _feature_major(cache_fox_v)),
              "sb": (_feature_major(cache_sb_k), _feature_major(cache_sb_v))}
    cmkt, cmvt = _feature_major(cache_mem_k), _feature_major(cache_mem_v)
    n_layers = {"fox": (depth + 1) // 2, "sb": depth // 2}
    kv_p = {"fox": None, "sb": None}
    fl_p = []
    fk_s, fv_s, fl_s, sk_s, sv_s = [], [], [], [], []
    for i in range(depth):
        j = i // 2
        is_fox = i % 2 == 0
        mode = "fox" if is_fox else "sb"
        if is_fox:
            wn, wt = _prep_w_in(w_in_fox[j], mix, n_heads)
            bias = jnp.pad(b_f[j], (0, FORGET_ROWS - n_heads)).reshape(FORGET_ROWS, 1)
        else:
            wn, wt = _prep_w_in(w_in_sb[j], mix, 0)
            bias = None
        w_gu, w_dn = _prep_ffn(w_gate_up[i], w_down[i])
        w_o = w_out[i].astype(BF16)
        g_fin = g_final if i == depth - 1 else None

        outs = _project(xp, g_mix[i], wn, wt, bias, batch=bp, mix=mix, memw=memw, tm=tm_p, tk=tk,
                        slot=j, n_slots=n_layers[mode], stacked=kv_p[mode])
        q, qm, kt, vt, ktb, vtb = outs[:6]
        kv_p[mode] = (kt, vt)
        c = None
        if is_fox:
            lft = outs[6]
            c = _key_bias_layout(_cumulative_logf(lft), n_heads, tk)
            fl_p.append(jnp.transpose(lft[:, :n_heads], (0, 2, 1)))
        o = _attention(q.reshape(bp, seq, mix), (ktb, vtb), c, mode=mode, tq=tq_p, tk=tk, ks=ks)
        xp_new = _post(xp, o.reshape(bp * seq, mix), qm, mkt_p, mvt_p, i,
                       w_o, g_ffn[i], w_gu, w_dn, g_fin, tm=tm_p, nb=1)

        outs = _project(xs, g_mix[i], wn, wt, bias, batch=1, mix=mix, memw=memw, tm=rows_s, tk=tk)
        q, qm, kt, vt = outs[:4]
        c = None
        if is_fox:
            lft = outs[6]
            lft_s = jnp.transpose(lft.reshape(FORGET_ROWS, bs, dec_seq), (1, 0, 2))
            lft_past = jnp.pad(jnp.transpose(cache_fox_logf[j], (0, 2, 1)),
                               ((0, 0), (0, FORGET_ROWS - n_heads), (0, 0)))
            lft_all = jnp.concatenate(
                [lft_past, jnp.pad(lft_s, ((0, 0), (0, 0), (0, tk - dec_seq)))], axis=2)
            c = _key_bias_layout(_cumulative_logf(lft_all), n_heads, tk)
            fl_s.append(jnp.transpose(lft_s[:, :n_heads], (0, 2, 1)))
        new_shape = (n_heads, HEAD_DIM, bs, dec_seq)
        k_new = jnp.transpose(kt.reshape(new_shape), (2, 3, 0, 1))
        v_new = jnp.transpose(vt.reshape(new_shape), (2, 3, 0, 1))
        if is_fox:
            fk_s.append(k_new)
            fv_s.append(v_new)
        else:
            sk_s.append(k_new)
            sv_s.append(v_new)
        o = _attention(q.reshape(bs, dec_seq, mix),
                       (*caches[mode], new_keys(kt), new_keys(vt)), c,
                       mode=mode, tq=dec_seq, tk=tk, ks=ks, past=past, cache_slot=j)
        xs_new = _post(xs, o.reshape(rows_s, mix), qm, cmkt, cmvt, i,
                       w_o, g_ffn[i], w_gu, w_dn, g_fin, tm=rows_s, nb=bs)
        xp, xs = xp_new, xs_new

    return (xp.reshape(bp, seq, d), xs.reshape(bs, dec_seq, d),
            _token_major(kv_p["fox"][0]), _token_major(kv_p["fox"][1]), jnp.stack(fl_p),
            _token_major(kv_p["sb"][0]), _token_major(kv_p["sb"][1]),
            _token_major(mkt_p), _token_major(mvt_p),
            jnp.stack(fk_s), jnp.stack(fv_s), jnp.stack(fl_s), jnp.stack(sk_s), jnp.stack(sv_s))
```

```python
import functools

import jax
import jax.numpy as jnp
from jax import lax
from jax.experimental import pallas as pl
from jax.experimental.pallas import tpu as pltpu

HEAD_DIM = 64
LANES = 128
HEADS_PER_TILE = LANES // HEAD_DIM
FORGET_ROWS = 16
EPS = 1e-6
LOG2E = 1.4426950408889634
QK_SCALE = HEAD_DIM ** -0.5 * LOG2E
NEG = -1e30
DEAD_LOG2 = 160.0
VMEM_LIMIT = 56 * 1024 * 1024

F32 = jnp.float32
BF16 = jnp.bfloat16
NT_DIMS = (((1,), (1,)), ((), ()))


def _rms_scale(x):
    return x * lax.rsqrt(jnp.mean(x * x, axis=-1, keepdims=True) + EPS)


def _const_spec(shape):
    return pl.BlockSpec(shape, lambda *_: (0,) * len(shape), pipeline_mode=pl.Buffered(1))


def _head_mask(rows, hh):
    lane = lax.broadcasted_iota(jnp.int32, (rows, LANES), 1)
    return (lane >= hh * HEAD_DIM) & (lane < (hh + 1) * HEAD_DIM)


def _proj_kernel(*refs, mix, tk, has_forget, n_aliased):
    refs = list(refs)
    x_ref, g_ref, wn_ref, wt_ref = refs[:4]
    del refs[:4]
    if has_forget:
        bf_ref = refs.pop(0)
    del refs[:n_aliased]
    q_ref, qm_ref, kt_ref, vt_ref, ktb_ref, vtb_ref = refs[:6]
    if has_forget:
        lft_ref = refs[6]
    tm = x_ref.shape[0]
    hn = (_rms_scale(x_ref[...]) * g_ref[...]).astype(BF16)
    qq = jnp.dot(hn, wn_ref[...], preferred_element_type=F32) * QK_SCALE
    q_ref[...] = qq[:, :mix].astype(BF16)
    qm_ref[...] = qq[:, mix:].astype(BF16)
    t = lax.dot_general(wt_ref[...], hn, NT_DIMS, preferred_element_type=F32)
    kt = t[:mix]
    vt = t[mix:2 * mix]
    kt_ref[0, 0] = kt
    vt_ref[0, 0] = vt
    for c in range(tm // tk):
        ktb_ref[0, c] = kt[:, c * tk:(c + 1) * tk].astype(BF16)
        vtb_ref[0, c] = vt[:, c * tk:(c + 1) * tk].astype(BF16)
    if has_forget:
        f = t[2 * mix:] + bf_ref[...]
        lft_ref[0] = jnp.minimum(f, 0.0) - jnp.log(1.0 + jnp.exp(-jnp.abs(f)))


def _project(x, g, wn, wt, bias, *, batch, mix, memw, tm, tk, slot=0, n_slots=1, stacked=None):
    rows, d = x.shape
    t_len = rows // batch
    per_batch = t_len // tm
    has_forget = bias is not None
    row = lambda width: pl.BlockSpec((tm, width), lambda r: (r, 0))
    feat = pl.BlockSpec((1, FORGET_ROWS, tm), lambda r: (r // per_batch, 0, r % per_batch))
    slotted = pl.BlockSpec((1, 1, mix, tm), lambda r: (slot, r // per_batch, 0, r % per_batch))
    blocked = pl.BlockSpec((1, tm // tk, mix, tk), lambda r: (r // per_batch, r % per_batch, 0, 0))
    in_specs = [row(d), _const_spec((1, d)), _const_spec(wn.shape), _const_spec(wt.shape)]
    args = [x, g.reshape(1, d), wn, wt]
    if has_forget:
        in_specs.append(_const_spec((FORGET_ROWS, 1)))
        args.append(bias)
    aliases = {}
    if stacked is not None:
        for a in stacked:
            aliases[len(args)] = 2 + len(aliases)
            in_specs.append(pl.BlockSpec(memory_space=pl.ANY))
            args.append(a)
    out_shape = [jax.ShapeDtypeStruct((rows, mix), BF16),
                 jax.ShapeDtypeStruct((rows, memw), BF16),
                 jax.ShapeDtypeStruct((n_slots, batch, mix, t_len), F32),
                 jax.ShapeDtypeStruct((n_slots, batch, mix, t_len), F32),
                 jax.ShapeDtypeStruct((batch, t_len // tk, mix, tk), BF16),
                 jax.ShapeDtypeStruct((batch, t_len // tk, mix, tk), BF16)]
    out_specs = [row(mix), row(memw), slotted, slotted, blocked, blocked]
    if has_forget:
        out_shape.append(jax.ShapeDtypeStruct((batch, FORGET_ROWS, t_len), F32))
        out_specs.append(feat)
    return pl.pallas_call(
        functools.partial(_proj_kernel, mix=mix, tk=tk, has_forget=has_forget,
                          n_aliased=len(aliases)),
        grid=(rows // tm,),
        in_specs=in_specs, out_specs=out_specs, out_shape=out_shape,
        input_output_aliases=aliases,
        compiler_params=pltpu.CompilerParams(
            dimension_semantics=("arbitrary",), vmem_limit_bytes=VMEM_LIMIT),
        name="proj_fox" if has_forget else "proj_sb",
    )(*args)


def _memkv_kernel(x_ref, g_ref, wt_ref, kt_ref, vt_ref, *, memw):
    hn = (_rms_scale(x_ref[0]) * g_ref[0]).astype(BF16)
    kvt = lax.dot_general(wt_ref[0], hn, NT_DIMS, preferred_element_type=F32)
    kt_ref[0, 0] = kvt[:memw]
    vt_ref[0, 0] = kvt[memw:]


def _memory_kv(mem, g_mem, wt_mem_kv):
    batch, n_mem, d = mem.shape
    depth, two_memw, _ = wt_mem_kv.shape
    memw = two_memw // 2
    out_spec = pl.BlockSpec((1, 1, memw, n_mem), lambda l, b: (l, b, 0, 0))
    return pl.pallas_call(
        functools.partial(_memkv_kernel, memw=memw),
        grid=(depth, batch),
        in_specs=[pl.BlockSpec((1, n_mem, d), lambda l, b: (b, 0, 0)),
                  pl.BlockSpec((1, 1, d), lambda l, b: (l, 0, 0)),
                  pl.BlockSpec((1, two_memw, d), lambda l, b: (l, 0, 0))],
        out_specs=[out_spec] * 2,
        out_shape=[jax.ShapeDtypeStruct((depth, batch, memw, n_mem), F32)] * 2,
        compiler_params=pltpu.CompilerParams(
            dimension_semantics=("arbitrary", "arbitrary"), vmem_limit_bytes=VMEM_LIMIT),
        name="memkv",
    )(mem, g_mem.reshape(depth, 1, d), wt_mem_kv)


def _split3(x):
    hi = x.astype(BF16)
    r1 = x - hi.astype(F32)
    mid = r1.astype(BF16)
    lo = (r1 - mid.astype(F32)).astype(BF16)
    return hi, mid, lo


def _cumsum_kernel(lf_ref, c_ref, *, t_len):
    r = lax.broadcasted_iota(jnp.int32, (LANES, LANES), 0)
    c = lax.broadcasted_iota(jnp.int32, (LANES, LANES), 1)
    upper = (r <= c).astype(BF16)
    carry = jnp.zeros((lf_ref.shape[1], 1), F32)
    for ch in range(t_len // LANES):
        cs = carry
        for part in _split3(lf_ref[0, :, ch * LANES:(ch + 1) * LANES]):
            cs = cs + jnp.dot(part, upper, preferred_element_type=F32)
        c_ref[0, :, ch * LANES:(ch + 1) * LANES] = cs * LOG2E
        carry = cs[:, LANES - 1:LANES]


def _cumulative_logf(lft):
    b, rows, t_len = lft.shape
    spec = pl.BlockSpec((1, rows, t_len), lambda i: (i, 0, 0))
    return pl.pallas_call(
        functools.partial(_cumsum_kernel, t_len=t_len),
        grid=(b,), in_specs=[spec], out_specs=spec,
        out_shape=jax.ShapeDtypeStruct((b, rows, t_len), F32),
        compiler_params=pltpu.CompilerParams(
            dimension_semantics=("arbitrary",), vmem_limit_bytes=VMEM_LIMIT),
        name="cumsum_logf",
    )(lft)


def _key_bias_layout(c, n_heads, tk):
    b, _, t_k = c.shape
    c = c[:, :n_heads].reshape(b, n_heads // HEADS_PER_TILE, HEADS_PER_TILE, t_k // tk, tk)
    c = jnp.transpose(c, (0, 1, 3, 2, 4))
    return jnp.pad(c, ((0, 0), (0, 0), (0, 0), (0, 8 - HEADS_PER_TILE), (0, 0)))


def _attn_kernel(*refs, mode, tq, tk, ks, past, cached):
    refs = list(refs)
    q_ref = refs.pop(0)
    if cached:
        ckt_ref, cvt_ref, nkt_ref, nvt_ref = refs[:4]
        refs = refs[4:]
    else:
        kt_ref, vt_ref = refs[:2]
        refs = refs[2:]
    c_ref = refs.pop(0) if mode == "fox" else None
    (o_ref,) = refs

    spb = ks // tk
    if cached:
        n_super = past // ks
        diag_blocks = [past // tk]

        def kv_block(j):
            if j == past // tk:
                return nkt_ref[0], nvt_ref[0]
            cols = slice(j * tk, (j + 1) * tk)
            return ckt_ref[0, 0, :, cols].astype(BF16), cvt_ref[0, 0, :, cols].astype(BF16)
    else:
        n_super = pl.program_id(2) * (tq // ks)
        diag_blocks = [n_super * spb + t for t in range(tq // tk)]

        def kv_block(j):
            return kt_ref[0, j], vt_ref[0, j]

    q = q_ref[0]
    qh = [jnp.where(_head_mask(tq, hh), q, jnp.zeros_like(q))
          for hh in range(HEADS_PER_TILE)]

    heads = range(HEADS_PER_TILE)
    groups = [(slice(d * tk, min((d + 1) * tk, tq)), diag_blocks[:d + 1], d * tk)
              for d in range(len(diag_blocks))]

    def take(x, rows):
        return x if rows == slice(None) else x[rows]

    def visible_mask(rows, width, koff, roff, strict):
        r = lax.broadcasted_iota(jnp.int32, (rows, width), 0) + roff
        c = lax.broadcasted_iota(jnp.int32, (rows, width), 1) + koff
        return c < r if strict else c <= r

    if mode == "fox":
        def scores(blocks, hh, rows=slice(None), roff=None):
            qr = take(qh[hh], rows)
            s = [jnp.dot(qr, kv_block(j)[0], preferred_element_type=F32) for j in blocks]
            cb = [c_ref[0, 0, j][hh:hh + 1, :] for j in blocks]
            s = (s[0] if len(s) == 1 else jnp.concatenate(s, axis=1))
            s = s - (cb[0] if len(cb) == 1 else jnp.concatenate(cb, axis=1))
            if roff is not None:
                s = jnp.where(visible_mask(s.shape[0], s.shape[1], 0, roff, strict=False), s, NEG)
            return s

        def softmax_step(s, m, l):
            m_new = jnp.maximum(m, jnp.max(s, axis=1, keepdims=True))
            alpha = jnp.exp2(m - m_new)
            p = jnp.exp2(s - m_new)
            lanes = p[:, :LANES]
            for t in range(1, s.shape[1] // LANES):
                lanes = lanes + p[:, t * LANES:(t + 1) * LANES]
            return m_new, alpha * l + lanes, alpha, p.astype(BF16)

        def weighted_values(acc, alpha, p, blocks):
            acc = alpha * acc
            for t, j in enumerate(blocks):
                acc = acc + lax.dot_general(p[:, t * tk:(t + 1) * tk], kv_block(j)[1], NT_DIMS,
                                            preferred_element_type=F32)
            return acc

        def step(carry, specs):
            m, l, acc = (list(carry[i::3]) for i in range(3))
            s = [[scores(blocks, hh, rows, roff) for hh in heads] for rows, blocks, roff in specs]
            new = [[None] * len(specs) for _ in heads]
            for g, (rows, blocks, _) in enumerate(specs):
                for hh in heads:
                    new[hh][g] = softmax_step(s[g][hh], take(m[hh], rows), take(l[hh], rows))
            out = []
            for hh in heads:
                parts = []
                for g, (rows, blocks, _) in enumerate(specs):
                    m_g, l_g, alpha, p = new[hh][g]
                    parts.append((m_g, l_g, weighted_values(take(acc[hh], rows), alpha, p, blocks)))
                out += [parts[0][i] if len(parts) == 1 else jnp.concatenate([x[i] for x in parts], axis=0)
                        for i in range(3)]
            return tuple(out)

        carry = (jnp.full((tq, 1), NEG, F32), jnp.zeros((tq, LANES), F32),
                 jnp.zeros((tq, LANES), F32)) * HEADS_PER_TILE
        whole = lambda blocks: [(slice(None), blocks, None)]
        if cached:
            if n_super:
                carry = step(carry, whole(list(range(n_super * spb))))
        else:
            carry = lax.fori_loop(
                0, n_super,
                lambda j, cy: step(cy, whole([j * spb + t for t in range(spb)])), carry)
        carry = step(carry, groups)
        head_out = [carry[3 * hh + 2] / jnp.sum(carry[3 * hh + 1], axis=1, keepdims=True)
                    for hh in heads]
    else:
        r = lax.broadcasted_iota(jnp.int32, (tk, tk), 0)
        c = lax.broadcasted_iota(jnp.int32, (tk, tk), 1)
        later = (r > c).astype(BF16)

        def update(state, blocks, rows=slice(None), roff=None):
            kv = [kv_block(j) for j in blocks]
            nrows = state[0].shape[0]
            visible = [None] * len(kv)
            if roff is not None:
                visible = [None if (t + 1) * tk <= roff
                           else visible_mask(nrows, tk, t * tk, roff, strict=True)
                           for t in range(len(kv))]
            out = []
            for hh in heads:
                run, acc = state[2 * hh], state[2 * hh + 1]
                qr = take(qh[hh], rows)
                logits, block_sums = [], []
                for t, (kt, _) in enumerate(kv):
                    z = jnp.dot(qr, kt, preferred_element_type=F32)
                    sp = jnp.maximum(z, 0.0) + jnp.log2(1.0 + jnp.exp2(-jnp.abs(z)))
                    if visible[t] is not None:
                        sp = jnp.where(visible[t], sp, 0.0)
                    rest = jnp.dot(sp.astype(BF16), later, preferred_element_type=F32)
                    logits.append(z - sp - rest)
                    block_sums.append(jnp.sum(sp, axis=1, keepdims=True))
                for t in reversed(range(len(kv))):
                    a = jnp.exp2(logits[t] - run)
                    if visible[t] is not None:
                        a = jnp.where(visible[t], a, 0.0)
                    acc = acc + lax.dot_general(a.astype(BF16), kv[t][1], NT_DIMS,
                                                preferred_element_type=F32)
                    run = run + block_sums[t]
                out += [run, acc]
            return tuple(out)

        def weights_vanish(state):
            run_min = jnp.minimum(jnp.min(state[0]), jnp.min(state[2]))
            return run_min > DEAD_LOG2

        state = (jnp.zeros((tq, 1), F32), jnp.zeros((tq, LANES), F32)) * HEADS_PER_TILE
        state = update(state, diag_blocks, roff=0)
        n_before = n_super * spb
        parts = []
        for rows, _, _ in groups:
            st = tuple(take(x, rows) for x in state)
            if cached:
                for j in reversed(range(n_before)):
                    st = lax.cond(weights_vanish(st), lambda s_: s_,
                                  lambda s_, j=j, rows=rows: update(s_, [j], rows), st)
            else:
                _, st = lax.while_loop(
                    lambda cy: (cy[0] < n_before) & jnp.logical_not(weights_vanish(cy[1])),
                    lambda cy, rows=rows: (cy[0] + 1, update(cy[1], [n_before - 1 - cy[0]], rows)),
                    (jnp.int32(0), st))
            parts.append(st)
        state = tuple(parts[0][i] if len(parts) == 1 else jnp.concatenate([x[i] for x in parts], axis=0)
                      for i in range(len(state)))
        head_out = [state[2 * hh + 1] for hh in heads]

    o_ref[0] = jnp.where(_head_mask(tq, 0), head_out[0], head_out[1]).astype(o_ref.dtype)


def _attention(q, kv, c, *, mode, tq, tk, ks, past=0, cache_slot=0):
    b, t_q, width = q.shape
    cached = len(kv) == 4
    pairs = width // LANES
    q_spec = pl.BlockSpec((1, tq, LANES), lambda bi, hp, qi: (bi, qi, hp))
    if cached:
        assert t_q == tq and tq <= tk and past % ks == 0 and ks % tk == 0
        old = pl.BlockSpec((1, 1, LANES, past), lambda bi, hp, qi: (cache_slot, bi, hp, 0))
        new = pl.BlockSpec((1, LANES, tk), lambda bi, hp, qi: (bi, hp, 0))
        kv_specs = [old, old, new, new]
    else:
        assert t_q % tq == 0 and tq % ks == 0 and ks % tk == 0 and past == 0
        blocked = pl.BlockSpec((1, t_q // tk, LANES, tk), lambda bi, hp, qi: (bi, 0, hp, 0))
        kv_specs = [blocked, blocked]
    in_specs = [q_spec] + kv_specs
    args = [q, *kv]
    if mode == "fox":
        in_specs.append(pl.BlockSpec((1, 1) + c.shape[2:], lambda bi, hp, qi: (bi, hp, 0, 0, 0)))
        args.append(c)
    return pl.pallas_call(
        functools.partial(_attn_kernel, mode=mode, tq=tq, tk=tk, ks=ks, past=past, cached=cached),
        grid=(b, pairs, t_q // tq),
        in_specs=in_specs,
        out_specs=q_spec,
        out_shape=jax.ShapeDtypeStruct((b, t_q, width), BF16),
        compiler_params=pltpu.CompilerParams(
            dimension_semantics=("arbitrary", "arbitrary", "arbitrary"),
            vmem_limit_bytes=VMEM_LIMIT),
        name=f"attn_{mode}",
    )(*args)


def _post_kernel(*refs, nb, mix, final):
    if final:
        (x_ref, o_ref, qm_ref, mkt_ref, mvt_ref, wo_ref, g_ref, wgu_ref, wd_ref, gf_ref,
         y_ref, acc_ref) = refs
    else:
        (x_ref, o_ref, qm_ref, mkt_ref, mvt_ref, wo_ref, g_ref, wgu_ref, wd_ref,
         xo_ref, acc_ref) = refs
    tm = x_ref.shape[0]
    rb = tm // nb
    memw = qm_ref.shape[1]

    om_rows = []
    for bi in range(nb):
        tiles = []
        for tl in range(memw // LANES):
            feats = slice(tl * LANES, (tl + 1) * LANES)
            qp = qm_ref[bi * rb:(bi + 1) * rb, feats]
            mkt = mkt_ref[0, bi, feats, :].astype(BF16)
            mvt = mvt_ref[0, bi, feats, :].astype(BF16)
            tile = None
            for hh in range(HEADS_PER_TILE):
                in_head = _head_mask(rb, hh)
                qh = jnp.where(in_head, qp, jnp.zeros_like(qp))
                s = jnp.dot(qh, mkt, preferred_element_type=F32)
                p = jnp.exp2(s - jnp.max(s, axis=1, keepdims=True))
                o = lax.dot_general(p.astype(BF16), mvt, NT_DIMS, preferred_element_type=F32)
                o = o / jnp.sum(p, axis=1, keepdims=True)
                tile = o if tile is None else jnp.where(in_head, o, tile)
            tiles.append(tile)
        om_rows.append(jnp.concatenate(tiles, axis=1))
    om = (om_rows[0] if nb == 1 else jnp.concatenate(om_rows, axis=0)).astype(BF16)

    x1 = (x_ref[...]
          + jnp.dot(o_ref[...], wo_ref[0:mix, :], preferred_element_type=F32)
          + jnp.dot(om, wo_ref[mix:mix + memw, :], preferred_element_type=F32))
    h = (_rms_scale(x1) * g_ref[...]).astype(BF16)
    ffc = wd_ref.shape[1]

    acc_ref[...] = x1

    def chunk(ci, _):
        gu = jnp.dot(h, wgu_ref[ci], preferred_element_type=F32)
        gate = gu[:, :ffc]
        act = (gate * jax.nn.sigmoid(gate) * gu[:, ffc:]).astype(BF16)
        acc_ref[...] += jnp.dot(act, wd_ref[ci], preferred_element_type=F32)
        return 0

    lax.fori_loop(0, wgu_ref.shape[0], chunk, 0)
    if final:
        y_ref[...] = _rms_scale(acc_ref[...]) * gf_ref[...]
    else:
        xo_ref[...] = acc_ref[...]


def _post(x, o_mix, qm, mkt, mvt, layer, w_out, g_ffn, w_gu, w_down, g_final, *, tm, nb):
    rows, d = x.shape
    mix = o_mix.shape[1]
    memw = qm.shape[1]
    n_mem = mkt.shape[3]
    final = g_final is not None
    rows_per_batch = rows // mkt.shape[1]
    assert tm == nb * rows_per_batch or (nb == 1 and rows_per_batch % tm == 0)
    blocks_per_batch = max(rows_per_batch // tm, 1)
    row = lambda width: pl.BlockSpec((tm, width), lambda r: (r, 0))
    mem_spec = pl.BlockSpec((1, nb, memw, n_mem), lambda r: (layer, r // blocks_per_batch, 0, 0))
    in_specs = [row(d), row(mix), row(memw), mem_spec, mem_spec,
                _const_spec(w_out.shape), _const_spec((1, d)),
                _const_spec(w_gu.shape), _const_spec(w_down.shape)]
    args = [x, o_mix, qm, mkt, mvt, w_out, g_ffn.reshape(1, d), w_gu, w_down]
    if final:
        in_specs.append(_const_spec((1, d)))
        args.append(g_final.reshape(1, d))
    return pl.pallas_call(
        functools.partial(_post_kernel, nb=nb, mix=mix, final=final),
        grid=(rows // tm,),
        in_specs=in_specs,
        out_specs=row(d),
        out_shape=jax.ShapeDtypeStruct((rows, d), F32),
        scratch_shapes=[pltpu.VMEM((tm, d), F32)],
        compiler_params=pltpu.CompilerParams(
            dimension_semantics=("arbitrary",), vmem_limit_bytes=VMEM_LIMIT),
        name="post_final" if final else "post",
    )(*args)


def _prep_w_in(w, mix, n_forget):
    wn = jnp.concatenate([w[:, :mix], w[:, 3 * mix + n_forget:]], axis=1)
    wt = w[:, mix:3 * mix + n_forget]
    if n_forget:
        wt = jnp.pad(wt, ((0, 0), (0, FORGET_ROWS - n_forget)))
    return wn.astype(BF16), wt.T.astype(BF16)


def _ff_chunk(d_ff):
    for c in (1408, 512, 256, 128):
        if d_ff % c == 0:
            return c
    raise ValueError(f"d_ff={d_ff} is not a multiple of {LANES}")


def _prep_ffn(w_gate_up, w_down):
    d, two_ff = w_gate_up.shape
    d_ff = two_ff // 2
    ffc = _ff_chunk(d_ff)
    n = d_ff // ffc
    gate = w_gate_up[:, :d_ff].reshape(d, n, ffc)
    up = w_gate_up[:, d_ff:].reshape(d, n, ffc)
    w_gu = jnp.transpose(jnp.concatenate([gate, up], axis=2), (1, 0, 2)).astype(BF16)
    return w_gu, w_down.reshape(n, ffc, d).astype(BF16)


def _feature_major(cache):
    *lead, t, h, hd = cache.shape
    return jnp.moveaxis(cache, -3, -1).reshape(*lead, h * hd, t)


def _token_major(xt):
    *lead, width, t = xt.shape
    return jnp.moveaxis(xt.reshape(*lead, width // HEAD_DIM, HEAD_DIM, t), -1, -3)


def kernel(x_prompt, x_sample, mem_prompt, cache_fox_k, cache_fox_v, cache_fox_logf, cache_sb_k, cache_sb_v, cache_mem_k, cache_mem_v, g_mix, w_in_fox, b_f, w_in_sb, g_mem, w_mem_kv, w_out, g_ffn, w_gate_up, w_down, g_final):
    bp, seq, d = x_prompt.shape
    bs, dec_seq, _ = x_sample.shape
    n_mem = mem_prompt.shape[1]
    depth = g_mix.shape[0]
    n_heads = b_f.shape[1]
    mix = n_heads * HEAD_DIM
    memw = w_mem_kv.shape[2] // 2
    past = cache_fox_k.shape[2]

    tk = 256
    ks = 512
    tq_p = ks
    tm_p = 512
    rows_s = bs * dec_seq
    assert seq % tq_p == 0 and seq % tm_p == 0 and rows_s % tk == 0
    assert dec_seq <= tk and past % ks == 0

    xp = x_prompt.reshape(bp * seq, d)
    xs = x_sample.reshape(rows_s, d)

    mkt_p, mvt_p = _memory_kv(mem_prompt, g_mem, jnp.transpose(w_mem_kv, (0, 2, 1)).astype(BF16))

    def new_keys(xt):
        xt = jnp.transpose(xt.reshape(mix, bs, dec_seq), (1, 0, 2))
        return jnp.pad(xt, ((0, 0), (0, 0), (0, tk - dec_seq))).astype(BF16)

    caches = {"fox": (_feature_major(cache_fox_k), _feature_major(cache_fox_v)),
              "sb": (_feature_major(cache_sb_k), _feature_major(cache_sb_v))}
    cmkt, cmvt = _feature_major(cache_mem_k), _feature_major(cache_mem_v)
    n_layers = {"fox": (depth + 1) // 2, "sb": depth // 2}
    kv_p = {"fox": None, "sb": None}
    fl_p = []
    fk_s, fv_s, fl_s, sk_s, sv_s = [], [], [], [], []
    for i in range(depth):
        j = i // 2
        is_fox = i % 2 == 0
        mode = "fox" if is_fox else "sb"
        if is_fox:
            wn, wt = _prep_w_in(w_in_fox[j], mix, n_heads)
            bias = jnp.pad(b_f[j], (0, FORGET_ROWS - n_heads)).reshape(FORGET_ROWS, 1)
        else:
            wn, wt = _prep_w_in(w_in_sb[j], mix, 0)
            bias = None
        w_gu, w_dn = _prep_ffn(w_gate_up[i], w_down[i])
        w_o = w_out[i].astype(BF16)
        g_fin = g_final if i == depth - 1 else None

        outs = _project(xp, g_mix[i], wn, wt, bias, batch=bp, mix=mix, memw=memw, tm=tm_p, tk=tk,
                        slot=j, n_slots=n_layers[mode], stacked=kv_p[mode])
        q, qm, kt, vt, ktb, vtb = outs[:6]
        kv_p[mode] = (kt, vt)
        c = None
        if is_fox:
            lft = outs[6]
            c = _key_bias_layout(_cumulative_logf(lft), n_heads, tk)
            fl_p.append(jnp.transpose(lft[:, :n_heads], (0, 2, 1)))
        o = _attention(q.reshape(bp, seq, mix), (ktb, vtb), c, mode=mode, tq=tq_p, tk=tk, ks=ks)
        xp_new = _post(xp, o.reshape(bp * seq, mix), qm, mkt_p, mvt_p, i,
                       w_o, g_ffn[i], w_gu, w_dn, g_fin, tm=tm_p, nb=1)

        outs = _project(xs, g_mix[i], wn, wt, bias, batch=1, mix=mix, memw=memw, tm=rows_s, tk=tk)
        q, qm, kt, vt = outs[:4]
        c = None
        if is_fox:
            lft = outs[6]
            lft_s = jnp.transpose(lft.reshape(FORGET_ROWS, bs, dec_seq), (1, 0, 2))
            lft_past = jnp.pad(jnp.transpose(cache_fox_logf[j], (0, 2, 1)),
                               ((0, 0), (0, FORGET_ROWS - n_heads), (0, 0)))
            lft_all = jnp.concatenate(
                [lft_past, jnp.pad(lft_s, ((0, 0), (0, 0), (0, tk - dec_seq)))], axis=2)
            c = _key_bias_layout(_cumulative_logf(lft_all), n_heads, tk)
            fl_s.append(jnp.transpose(lft_s[:, :n_heads], (0, 2, 1)))
        new_shape = (n_heads, HEAD_DIM, bs, dec_seq)
        k_new = jnp.transpose(kt.reshape(new_shape), (2, 3, 0, 1))
        v_new = jnp.transpose(vt.reshape(new_shape), (2, 3, 0, 1))
        if is_fox:
            fk_s.append(k_new)
            fv_s.append(v_new)
        else:
            sk_s.append(k_new)
            sv_s.append(v_new)
        o = _attention(q.reshape(bs, dec_seq, mix),
                       (*caches[mode], new_keys(kt), new_keys(vt)), c,
                       mode=mode, tq=dec_seq, tk=tk, ks=ks, past=past, cache_slot=j)
        xs_new = _post(xs, o.reshape(rows_s, mix), qm, cmkt, cmvt, i,
                       w_o, g_ffn[i], w_gu, w_dn, g_fin, tm=rows_s, nb=bs)
        xp, xs = xp_new, xs_new

    return (xp.reshape(bp, seq, d), xs.reshape(bs, dec_seq, d),
            _token_major(kv_p["fox"][0]), _token_major(kv_p["fox"][1]), jnp.stack(fl_p),
            _token_major(kv_p["sb"][0]), _token_major(kv_p["sb"][1]),
            _token_major(mkt_p), _token_major(mvt_p),
            jnp.stack(fk_s), jnp.stack(fv_s), jnp.stack(fl_s), jnp.stack(sk_s), jnp.stack(sv_s))
```

```python
import functools

import jax
import jax.numpy as jnp
from jax import lax
from jax.experimental import pallas as pl
from jax.experimental.pallas import tpu as pltpu

HEAD_DIM = 64
LANES = 128
HEADS_PER_TILE = LANES // HEAD_DIM
FORGET_ROWS = 16
EPS = 1e-6
LOG2E = 1.4426950408889634
QK_SCALE = HEAD_DIM ** -0.5 * LOG2E
NEG = -1e30
DEAD_LOG2 = 160.0
VMEM_LIMIT = 56 * 1024 * 1024

F32 = jnp.float32
BF16 = jnp.bfloat16
NT_DIMS = (((1,), (1,)), ((), ()))


def _rms_scale(x):
    return x * lax.rsqrt(jnp.mean(x * x, axis=-1, keepdims=True) + EPS)


def _const_spec(shape):
    return pl.BlockSpec(shape, lambda *_: (0,) * len(shape), pipeline_mode=pl.Buffered(1))


def _head_mask(rows, hh):
    lane = lax.broadcasted_iota(jnp.int32, (rows, LANES), 1)
    return (lane >= hh * HEAD_DIM) & (lane < (hh + 1) * HEAD_DIM)


def _proj_kernel(*refs, mix, tk, has_forget, n_aliased):
    refs = list(refs)
    x_ref, g_ref, wn_ref, wt_ref = refs[:4]
    del refs[:4]
    if has_forget:
        bf_ref = refs.pop(0)
    del refs[:n_aliased]
    q_ref, qm_ref, kt_ref, vt_ref, ktb_ref, vtb_ref = refs[:6]
    if has_forget:
        lft_ref = refs[6]
    tm = x_ref.shape[0]
    hn = (_rms_scale(x_ref[...]) * g_ref[...]).astype(BF16)
    qq = jnp.dot(hn, wn_ref[...], preferred_element_type=F32) * QK_SCALE
    q_ref[...] = qq[:, :mix].astype(BF16)
    qm_ref[...] = qq[:, mix:].astype(BF16)
    t = lax.dot_general(wt_ref[...], hn, NT_DIMS, preferred_element_type=F32)
    kt = t[:mix]
    vt = t[mix:2 * mix]
    kt_ref[0, 0] = kt
    vt_ref[0, 0] = vt
    for c in range(tm // tk):
        ktb_ref[0, c] = kt[:, c * tk:(c + 1) * tk].astype(BF16)
        vtb_ref[0, c] = vt[:, c * tk:(c + 1) * tk].astype(BF16)
    if has_forget:
        f = t[2 * mix:] + bf_ref[...]
        lft_ref[0] = jnp.minimum(f, 0.0) - jnp.log(1.0 + jnp.exp(-jnp.abs(f)))


def _project(x, g, wn, wt, bias, *, batch, mix, memw, tm, tk, slot=0, n_slots=1, stacked=None):
    rows, d = x.shape
    t_len = rows // batch
    per_batch = t_len // tm
    has_forget = bias is not None
    row = lambda width: pl.BlockSpec((tm, width), lambda r: (r, 0))
    feat = pl.BlockSpec((1, FORGET_ROWS, tm), lambda r: (r // per_batch, 0, r % per_batch))
    slotted = pl.BlockSpec((1, 1, mix, tm), lambda r: (slot, r // per_batch, 0, r % per_batch))
    blocked = pl.BlockSpec((1, tm // tk, mix, tk), lambda r: (r // per_batch, r % per_batch, 0, 0))
    in_specs = [row(d), _const_spec((1, d)), _const_spec(wn.shape), _const_spec(wt.shape)]
    args = [x, g.reshape(1, d), wn, wt]
    if has_forget:
        in_specs.append(_const_spec((FORGET_ROWS, 1)))
        args.append(bias)
    aliases = {}
    if stacked is not None:
        for a in stacked:
            aliases[len(args)] = 2 + len(aliases)
            in_specs.append(pl.BlockSpec(memory_space=pl.ANY))
            args.append(a)
    out_shape = [jax.ShapeDtypeStruct((rows, mix), BF16),
                 jax.ShapeDtypeStruct((rows, memw), BF16),
                 jax.ShapeDtypeStruct((n_slots, batch, mix, t_len), F32),
                 jax.ShapeDtypeStruct((n_slots, batch, mix, t_len), F32),
                 jax.ShapeDtypeStruct((batch, t_len // tk, mix, tk), BF16),
                 jax.ShapeDtypeStruct((batch, t_len // tk, mix, tk), BF16)]
    out_specs = [row(mix), row(memw), slotted, slotted, blocked, blocked]
    if has_forget:
        out_shape.append(jax.ShapeDtypeStruct((batch, FORGET_ROWS, t_len), F32))
        out_specs.append(feat)
    return pl.pallas_call(
        functools.partial(_proj_kernel, mix=mix, tk=tk, has_forget=has_forget,
                          n_aliased=len(aliases)),
        grid=(rows // tm,),
        in_specs=in_specs, out_specs=out_specs, out_shape=out_shape,
        input_output_aliases=aliases,
        compiler_params=pltpu.CompilerParams(
            dimension_semantics=("arbitrary",), vmem_limit_bytes=VMEM_LIMIT),
        name="proj_fox" if has_forget else "proj_sb",
    )(*args)


def _memkv_kernel(x_ref, g_ref, wt_ref, kt_ref, vt_ref, *, memw):
    hn = (_rms_scale(x_ref[0]) * g_ref[0]).astype(BF16)
    kvt = lax.dot_general(wt_ref[0], hn, NT_DIMS, preferred_element_type=F32)
    kt_ref[0, 0] = kvt[:memw]
    vt_ref[0, 0] = kvt[memw:]


def _memory_kv(mem, g_mem, wt_mem_kv):
    batch, n_mem, d = mem.shape
    depth, two_memw, _ = wt_mem_kv.shape
    memw = two_memw // 2
    out_spec = pl.BlockSpec((1, 1, memw, n_mem), lambda l, b: (l, b, 0, 0))
    return pl.pallas_call(
        functools.partial(_memkv_kernel, memw=memw),
        grid=(depth, batch),
        in_specs=[pl.BlockSpec((1, n_mem, d), lambda l, b: (b, 0, 0)),
                  pl.BlockSpec((1, 1, d), lambda l, b: (l, 0, 0)),
                  pl.BlockSpec((1, two_memw, d), lambda l, b: (l, 0, 0))],
        out_specs=[out_spec] * 2,
        out_shape=[jax.ShapeDtypeStruct((depth, batch, memw, n_mem), F32)] * 2,
        compiler_params=pltpu.CompilerParams(
            dimension_semantics=("arbitrary", "arbitrary"), vmem_limit_bytes=VMEM_LIMIT),
        name="memkv",
    )(mem, g_mem.reshape(depth, 1, d), wt_mem_kv)


def _split3(x):
    hi = x.astype(BF16)
    r1 = x - hi.astype(F32)
    mid = r1.astype(BF16)
    lo = (r1 - mid.astype(F32)).astype(BF16)
    return hi, mid, lo


def _cumsum_kernel(lf_ref, c_ref, *, t_len):
    r = lax.broadcasted_iota(jnp.int32, (LANES, LANES), 0)
    c = lax.broadcasted_iota(jnp.int32, (LANES, LANES), 1)
    upper = (r <= c).astype(BF16)
    carry = jnp.zeros((lf_ref.shape[1], 1), F32)
    for ch in range(t_len // LANES):
        cs = carry
        for part in _split3(lf_ref[0, :, ch * LANES:(ch + 1) * LANES]):
            cs = cs + jnp.dot(part, upper, preferred_element_type=F32)
        c_ref[0, :, ch * LANES:(ch + 1) * LANES] = cs * LOG2E
        carry = cs[:, LANES - 1:LANES]


def _cumulative_logf(lft):
    b, rows, t_len = lft.shape
    spec = pl.BlockSpec((1, rows, t_len), lambda i: (i, 0, 0))
    return pl.pallas_call(
        functools.partial(_cumsum_kernel, t_len=t_len),
        grid=(b,), in_specs=[spec], out_specs=spec,
        out_shape=jax.ShapeDtypeStruct((b, rows, t_len), F32),
        compiler_params=pltpu.CompilerParams(
            dimension_semantics=("arbitrary",), vmem_limit_bytes=VMEM_LIMIT),
        name="cumsum_logf",
    )(lft)


def _key_bias_layout(c, n_heads, tk):
    b, _, t_k = c.shape
    c = c[:, :n_heads].reshape(b, n_heads // HEADS_PER_TILE, HEADS_PER_TILE, t_k // tk, tk)
    c = jnp.transpose(c, (0, 1, 3, 2, 4))
    return jnp.pad(c, ((0, 0), (0, 0), (0, 0), (0, 8 - HEADS_PER_TILE), (0, 0)))


def _attn_kernel(*refs, mode, tq, tk, ks, past, cached):
    refs = list(refs)
    q_ref = refs.pop(0)
    if cached:
        ckt_ref, cvt_ref, nkt_ref, nvt_ref = refs[:4]
        refs = refs[4:]
    else:
        kt_ref, vt_ref = refs[:2]
        refs = refs[2:]
    c_ref = refs.pop(0) if mode == "fox" else None
    (o_ref,) = refs

    spb = ks // tk
    if cached:
        n_super = past // ks
        diag_blocks = [past // tk]

        def kv_block(j):
            if j == past // tk:
                return nkt_ref[0], nvt_ref[0]
            cols = slice(j * tk, (j + 1) * tk)
            return ckt_ref[0, 0, :, cols].astype(BF16), cvt_ref[0, 0, :, cols].astype(BF16)
    else:
        n_super = pl.program_id(2) * (tq // ks)
        diag_blocks = [n_super * spb + t for t in range(tq // tk)]

        def kv_block(j):
            return kt_ref[0, j], vt_ref[0, j]

    q = q_ref[0]
    qh = [jnp.where(_head_mask(tq, hh), q, jnp.zeros_like(q))
          for hh in range(HEADS_PER_TILE)]

    heads = range(HEADS_PER_TILE)
    groups = [(slice(d * tk, min((d + 1) * tk, tq)), diag_blocks[:d + 1], d * tk)
              for d in range(len(diag_blocks))]

    def take(x, rows):
        return x if rows == slice(None) else x[rows]

    def visible_mask(rows, width, koff, roff, strict):
        r = lax.broadcasted_iota(jnp.int32, (rows, width), 0) + roff
        c = lax.broadcasted_iota(jnp.int32, (rows, width), 1) + koff
        return c < r if strict else c <= r

    if mode == "fox":
        def scores(blocks, hh, rows=slice(None), roff=None):
            qr = take(qh[hh], rows)
            s = [jnp.dot(qr, kv_block(j)[0], preferred_element_type=F32) for j in blocks]
            cb = [c_ref[0, 0, j][hh:hh + 1, :] for j in blocks]
            s = (s[0] if len(s) == 1 else jnp.concatenate(s, axis=1))
            s = s - (cb[0] if len(cb) == 1 else jnp.concatenate(cb, axis=1))
            if roff is not None:
                s = jnp.where(visible_mask(s.shape[0], s.shape[1], 0, roff, strict=False), s, NEG)
            return s

        def softmax_step(s, m, l):
            m_new = jnp.maximum(m, jnp.max(s, axis=1, keepdims=True))
            alpha = jnp.exp2(m - m_new)
            p = jnp.exp2(s - m_new)
            lanes = p[:, :LANES]
            for t in range(1, s.shape[1] // LANES):
                lanes = lanes + p[:, t * LANES:(t + 1) * LANES]
            return m_new, alpha * l + lanes, alpha, p.astype(BF16)

        def weighted_values(acc, alpha, p, blocks):
            acc = alpha * acc
            for t, j in enumerate(blocks):
                acc = acc + lax.dot_general(p[:, t * tk:(t + 1) * tk], kv_block(j)[1], NT_DIMS,
                                            preferred_element_type=F32)
            return acc

        def step(carry, specs):
            m, l, acc = (list(carry[i::3]) for i in range(3))
            s = [[scores(blocks, hh, rows, roff) for hh in heads] for rows, blocks, roff in specs]
            new = [[None] * len(specs) for _ in heads]
            for g, (rows, blocks, _) in enumerate(specs):
                for hh in heads:
                    new[hh][g] = softmax_step(s[g][hh], take(m[hh], rows), take(l[hh], rows))
            out = []
            for hh in heads:
                parts = []
                for g, (rows, blocks, _) in enumerate(specs):
                    m_g, l_g, alpha, p = new[hh][g]
                    parts.append((m_g, l_g, weighted_values(take(acc[hh], rows), alpha, p, blocks)))
                out += [parts[0][i] if len(parts) == 1 else jnp.concatenate([x[i] for x in parts], axis=0)
                        for i in range(3)]
            return tuple(out)

        carry = (jnp.full((tq, 1), NEG, F32), jnp.zeros((tq, LANES), F32),
                 jnp.zeros((tq, LANES), F32)) * HEADS_PER_TILE
        whole = lambda blocks: [(slice(None), blocks, None)]
        if cached:
            if n_super:
                carry = step(carry, whole(list(range(n_super * spb))))
        else:
            carry = lax.fori_loop(
                0, n_super,
                lambda j, cy: step(cy, whole([j * spb + t for t in range(spb)])), carry)
        carry = step(carry, groups)
        head_out = [carry[3 * hh + 2] / jnp.sum(carry[3 * hh + 1], axis=1, keepdims=True)
                    for hh in heads]
    else:
        r = lax.broadcasted_iota(jnp.int32, (tk, tk), 0)
        c = lax.broadcasted_iota(jnp.int32, (tk, tk), 1)
        later = (r > c).astype(BF16)

        def update(state, blocks, rows=slice(None), roff=None):
            kv = [kv_block(j) for j in blocks]
            nrows = state[0].shape[0]
            visible = [None] * len(kv)
            if roff is not None:
                visible = [None if (t + 1) * tk <= roff
                           else visible_mask(nrows, tk, t * tk, roff, strict=True)
                           for t in range(len(kv))]
            out = []
            for hh in heads:
                run, acc = state[2 * hh], state[2 * hh + 1]
                qr = take(qh[hh], rows)
                logits, block_sums = [], []
                for t, (kt, _) in enumerate(kv):
                    z = jnp.dot(qr, kt, preferred_element_type=F32)
                    sp = jnp.maximum(z, 0.0) + jnp.log2(1.0 + jnp.exp2(-jnp.abs(z)))
                    if visible[t] is not None:
                        sp = jnp.where(visible[t], sp, 0.0)
                    rest = jnp.dot(sp.astype(BF16), later, preferred_element_type=F32)
                    logits.append(z - sp - rest)
                    block_sums.append(jnp.sum(sp, axis=1, keepdims=True))
                for t in reversed(range(len(kv))):
                    a = jnp.exp2(logits[t] - run)
                    if visible[t] is not None:
                        a = jnp.where(visible[t], a, 0.0)
                    acc = acc + lax.dot_general(a.astype(BF16), kv[t][1], NT_DIMS,
                                                preferred_element_type=F32)
                    run = run + block_sums[t]
                out += [run, acc]
            return tuple(out)

        def weights_vanish(state):
            run_min = jnp.minimum(jnp.min(state[0]), jnp.min(state[2]))
            return run_min > DEAD_LOG2

        state = (jnp.zeros((tq, 1), F32), jnp.zeros((tq, LANES), F32)) * HEADS_PER_TILE
        state = update(state, diag_blocks, roff=0)
        n_before = n_super * spb
        parts = []
        for rows, _, _ in groups:
            st = tuple(take(x, rows) for x in state)
            if cached:
                for j in reversed(range(n_before)):
                    st = lax.cond(weights_vanish(st), lambda s_: s_,
                                  lambda s_, j=j, rows=rows: update(s_, [j], rows), st)
            else:
                _, st = lax.while_loop(
                    lambda cy: (cy[0] < n_before) & jnp.logical_not(weights_vanish(cy[1])),
                    lambda cy, rows=rows: (cy[0] + 1, update(cy[1], [n_before - 1 - cy[0]], rows)),
                    (jnp.int32(0), st))
            parts.append(st)
        state = tuple(parts[0][i] if len(parts) == 1 else jnp.concatenate([x[i] for x in parts], axis=0)
                      for i in range(len(state)))
        head_out = [state[2 * hh + 1] for hh in heads]

    o_ref[0] = jnp.where(_head_mask(tq, 0), head_out[0], head_out[1]).astype(o_ref.dtype)


def _attention(q, kv, c, *, mode, tq, tk, ks, past=0, cache_slot=0):
    b, t_q, width = q.shape
    cached = len(kv) == 4
    pairs = width // LANES
    q_spec = pl.BlockSpec((1, tq, LANES), lambda bi, hp, qi: (bi, qi, hp))
    if cached:
        assert t_q == tq and tq <= tk and past % ks == 0 and ks % tk == 0
        old = pl.BlockSpec((1, 1, LANES, past), lambda bi, hp, qi: (cache_slot, bi, hp, 0))
        new = pl.BlockSpec((1, LANES, tk), lambda bi, hp, qi: (bi, hp, 0))
        kv_specs = [old, old, new, new]
    else:
        assert t_q % tq == 0 and tq % ks == 0 and ks % tk == 0 and past == 0
        blocked = pl.BlockSpec((1, t_q // tk, LANES, tk), lambda bi, hp, qi: (bi, 0, hp, 0))
        kv_specs = [blocked, blocked]
    in_specs = [q_spec] + kv_specs
    args = [q, *kv]
    if mode == "fox":
        in_specs.append(pl.BlockSpec((1, 1) + c.shape[2:], lambda bi, hp, qi: (bi, hp, 0, 0, 0)))
        args.append(c)
    return pl.pallas_call(
        functools.partial(_attn_kernel, mode=mode, tq=tq, tk=tk, ks=ks, past=past, cached=cached),
        grid=(b, pairs, t_q // tq),
        in_specs=in_specs,
        out_specs=q_spec,
        out_shape=jax.ShapeDtypeStruct((b, t_q, width), BF16),
        compiler_params=pltpu.CompilerParams(
            dimension_semantics=("arbitrary", "arbitrary", "arbitrary"),
            vmem_limit_bytes=VMEM_LIMIT),
        name=f"attn_{mode}",
    )(*args)


def _post_kernel(*refs, nb, mix, final):
    if final:
        (x_ref, o_ref, qm_ref, mkt_ref, mvt_ref, wo_ref, g_ref, wgu_ref, wd_ref, gf_ref,
         y_ref, acc_ref) = refs
    else:
        (x_ref, o_ref, qm_ref, mkt_ref, mvt_ref, wo_ref, g_ref, wgu_ref, wd_ref,
         xo_ref, acc_ref) = refs
    tm = x_ref.shape[0]
    rb = tm // nb
    memw = qm_ref.shape[1]

    om_rows = []
    for bi in range(nb):
        tiles = []
        for tl in range(memw // LANES):
            feats = slice(tl * LANES, (tl + 1) * LANES)
            qp = qm_ref[bi * rb:(bi + 1) * rb, feats]
            mkt = mkt_ref[0, bi, feats, :].astype(BF16)
            mvt = mvt_ref[0, bi, feats, :].astype(BF16)
            tile = None
            for hh in range(HEADS_PER_TILE):
                in_head = _head_mask(rb, hh)
                qh = jnp.where(in_head, qp, jnp.zeros_like(qp))
                s = jnp.dot(qh, mkt, preferred_element_type=F32)
                p = jnp.exp2(s - jnp.max(s, axis=1, keepdims=True))
                o = lax.dot_general(p.astype(BF16), mvt, NT_DIMS, preferred_element_type=F32)
                o = o / jnp.sum(p, axis=1, keepdims=True)
                tile = o if tile is None else jnp.where(in_head, o, tile)
            tiles.append(tile)
        om_rows.append(jnp.concatenate(tiles, axis=1))
    om = (om_rows[0] if nb == 1 else jnp.concatenate(om_rows, axis=0)).astype(BF16)

    x1 = (x_ref[...]
          + jnp.dot(o_ref[...], wo_ref[0:mix, :], preferred_element_type=F32)
          + jnp.dot(om, wo_ref[mix:mix + memw, :], preferred_element_type=F32))
    h = (_rms_scale(x1) * g_ref[...]).astype(BF16)
    ffc = wd_ref.shape[1]

    acc_ref[...] = x1

    def chunk(ci, _):
        gu = jnp.dot(h, wgu_ref[ci], preferred_element_type=F32)
        gate = gu[:, :ffc]
        act = (gate * jax.nn.sigmoid(gate) * gu[:, ffc:]).astype(BF16)
        acc_ref[...] += jnp.dot(act, wd_ref[ci], preferred_element_type=F32)
        return 0

    lax.fori_loop(0, wgu_ref.shape[0], chunk, 0)
    if final:
        y_ref[...] = _rms_scale(acc_ref[...]) * gf_ref[...]
    else:
        xo_ref[...] = acc_ref[...]


def _post(x, o_mix, qm, mkt, mvt, layer, w_out, g_ffn, w_gu, w_down, g_final, *, tm, nb):
    rows, d = x.shape
    mix = o_mix.shape[1]
    memw = qm.shape[1]
    n_mem = mkt.shape[3]
    final = g_final is not None
    rows_per_batch = rows // mkt.shape[1]
    assert tm == nb * rows_per_batch or (nb == 1 and rows_per_batch % tm == 0)
    blocks_per_batch = max(rows_per_batch // tm, 1)
    row = lambda width: pl.BlockSpec((tm, width), lambda r: (r, 0))
    mem_spec = pl.BlockSpec((1, nb, memw, n_mem), lambda r: (layer, r // blocks_per_batch, 0, 0))
    in_specs = [row(d), row(mix), row(memw), mem_spec, mem_spec,
                _const_spec(w_out.shape), _const_spec((1, d)),
                _const_spec(w_gu.shape), _const_spec(w_down.shape)]
    args = [x, o_mix, qm, mkt, mvt, w_out, g_ffn.reshape(1, d), w_gu, w_down]
    if final:
        in_specs.append(_const_spec((1, d)))
        args.append(g_final.reshape(1, d))
    return pl.pallas_call(
        functools.partial(_post_kernel, nb=nb, mix=mix, final=final),
        grid=(rows // tm,),
        in_specs=in_specs,
        out_specs=row(d),
        out_shape=jax.ShapeDtypeStruct((rows, d), F32),
        scratch_shapes=[pltpu.VMEM((tm, d), F32)],
        compiler_params=pltpu.CompilerParams(
            dimension_semantics=("arbitrary",), vmem_limit_bytes=VMEM_LIMIT),
        name="post_final" if final else "post",
    )(*args)


def _prep_w_in(w, mix, n_forget):
    wn = jnp.concatenate([w[:, :mix], w[:, 3 * mix + n_forget:]], axis=1)
    wt = w[:, mix:3 * mix + n_forget]
    if n_forget:
        wt = jnp.pad(wt, ((0, 0), (0, FORGET_ROWS - n_forget)))
    return wn.astype(BF16), wt.T.astype(BF16)


def _ff_chunk(d_ff):
    for c in (1408, 512, 256, 128):
        if d_ff % c == 0:
            return c
    raise ValueError(f"d_ff={d_ff} is not a multiple of {LANES}")


def _prep_ffn(w_gate_up, w_down):
    layers, d, two_ff = w_gate_up.shape
    d_ff = two_ff // 2
    ffc = _ff_chunk(d_ff)
    n = d_ff // ffc
    w_gu = jnp.transpose(w_gate_up.reshape(layers, d, 2, n, ffc), (0, 3, 1, 2, 4))
    return (w_gu.reshape(layers, n, d, 2 * ffc).astype(BF16),
            w_down.reshape(layers, n, ffc, d).astype(BF16))


def _feature_major(cache):
    *lead, t, h, hd = cache.shape
    return jnp.moveaxis(cache, -3, -1).reshape(*lead, h * hd, t)


def _token_major(xt):
    *lead, width, t = xt.shape
    return jnp.moveaxis(xt.reshape(*lead, width // HEAD_DIM, HEAD_DIM, t), -1, -3)


def kernel(x_prompt, x_sample, mem_prompt, cache_fox_k, cache_fox_v, cache_fox_logf, cache_sb_k, cache_sb_v, cache_mem_k, cache_mem_v, g_mix, w_in_fox, b_f, w_in_sb, g_mem, w_mem_kv, w_out, g_ffn, w_gate_up, w_down, g_final):
    bp, seq, d = x_prompt.shape
    bs, dec_seq, _ = x_sample.shape
    n_mem = mem_prompt.shape[1]
    depth = g_mix.shape[0]
    n_heads = b_f.shape[1]
    mix = n_heads * HEAD_DIM
    memw = w_mem_kv.shape[2] // 2
    past = cache_fox_k.shape[2]

    tk = 256
    ks = 512
    tq_p = ks
    tm_p = 512
    rows_s = bs * dec_seq
    assert seq % tq_p == 0 and seq % tm_p == 0 and rows_s % tk == 0
    assert dec_seq <= tk and past % ks == 0

    xp = x_prompt.reshape(bp * seq, d)
    xs = x_sample.reshape(rows_s, d)

    mkt_p, mvt_p = _memory_kv(mem_prompt, g_mem, jnp.transpose(w_mem_kv, (0, 2, 1)).astype(BF16))

    def new_keys(xt):
        xt = jnp.transpose(xt.reshape(mix, bs, dec_seq), (1, 0, 2))
        return jnp.pad(xt, ((0, 0), (0, 0), (0, tk - dec_seq))).astype(BF16)

    w_gu_all, w_dn_all = _prep_ffn(w_gate_up, w_down)
    w_o_all = w_out.astype(BF16)
    caches = {"fox": (_feature_major(cache_fox_k), _feature_major(cache_fox_v)),
              "sb": (_feature_major(cache_sb_k), _feature_major(cache_sb_v))}
    cmkt, cmvt = _feature_major(cache_mem_k), _feature_major(cache_mem_v)
    n_layers = {"fox": (depth + 1) // 2, "sb": depth // 2}
    kv_p = {"fox": None, "sb": None}
    fl_p = []
    fk_s, fv_s, fl_s, sk_s, sv_s = [], [], [], [], []
    for i in range(depth):
        j = i // 2
        is_fox = i % 2 == 0
        mode = "fox" if is_fox else "sb"
        if is_fox:
            wn, wt = _prep_w_in(w_in_fox[j], mix, n_heads)
            bias = jnp.pad(b_f[j], (0, FORGET_ROWS - n_heads)).reshape(FORGET_ROWS, 1)
        else:
            wn, wt = _prep_w_in(w_in_sb[j], mix, 0)
            bias = None
        w_gu, w_dn, w_o = w_gu_all[i], w_dn_all[i], w_o_all[i]
        g_fin = g_final if i == depth - 1 else None

        outs = _project(xp, g_mix[i], wn, wt, bias, batch=bp, mix=mix, memw=memw, tm=tm_p, tk=tk,
                        slot=j, n_slots=n_layers[mode], stacked=kv_p[mode])
        q, qm, kt, vt, ktb, vtb = outs[:6]
        kv_p[mode] = (kt, vt)
        c = None
        if is_fox:
            lft = outs[6]
            c = _key_bias_layout(_cumulative_logf(lft), n_heads, tk)
            fl_p.append(jnp.transpose(lft[:, :n_heads], (0, 2, 1)))
        o = _attention(q.reshape(bp, seq, mix), (ktb, vtb), c, mode=mode, tq=tq_p, tk=tk, ks=ks)
        xp_new = _post(xp, o.reshape(bp * seq, mix), qm, mkt_p, mvt_p, i,
                       w_o, g_ffn[i], w_gu, w_dn, g_fin, tm=tm_p, nb=1)

        outs = _project(xs, g_mix[i], wn, wt, bias, batch=1, mix=mix, memw=memw, tm=rows_s, tk=tk)
        q, qm, kt, vt = outs[:4]
        c = None
        if is_fox:
            lft = outs[6]
            lft_s = jnp.transpose(lft.reshape(FORGET_ROWS, bs, dec_seq), (1, 0, 2))
            lft_past = jnp.pad(jnp.transpose(cache_fox_logf[j], (0, 2, 1)),
                               ((0, 0), (0, FORGET_ROWS - n_heads), (0, 0)))
            lft_all = jnp.concatenate(
                [lft_past, jnp.pad(lft_s, ((0, 0), (0, 0), (0, tk - dec_seq)))], axis=2)
            c = _key_bias_layout(_cumulative_logf(lft_all), n_heads, tk)
            fl_s.append(jnp.transpose(lft_s[:, :n_heads], (0, 2, 1)))
        new_shape = (n_heads, HEAD_DIM, bs, dec_seq)
        k_new = jnp.transpose(kt.reshape(new_shape), (2, 3, 0, 1))
        v_new = jnp.transpose(vt.reshape(new_shape), (2, 3, 0, 1))
        if is_fox:
            fk_s.append(k_new)
            fv_s.append(v_new)
        else:
            sk_s.append(k_new)
            sv_s.append(v_new)
        o = _attention(q.reshape(bs, dec_seq, mix),
                       (*caches[mode], new_keys(kt), new_keys(vt)), c,
                       mode=mode, tq=dec_seq, tk=tk, ks=ks, past=past, cache_slot=j)
        xs_new = _post(xs, o.reshape(rows_s, mix), qm, cmkt, cmvt, i,
                       w_o, g_ffn[i], w_gu, w_dn, g_fin, tm=rows_s, nb=bs)
        xp, xs = xp_new, xs_new

    return (xp.reshape(bp, seq, d), xs.reshape(bs, dec_seq, d),
            _token_major(kv_p["fox"][0]), _token_major(kv_p["fox"][1]), jnp.stack(fl_p),
            _token_major(kv_p["sb"][0]), _token_major(kv_p["sb"][1]),
            _token_major(mkt_p), _token_major(mvt_p),
            jnp.stack(fk_s), jnp.stack(fv_s), jnp.stack(fl_s), jnp.stack(sk_s), jnp.stack(sv_s))
```

```python
import functools

import jax
import jax.numpy as jnp
from jax import lax
from jax.experimental import pallas as pl
from jax.experimental.pallas import tpu as pltpu

HEAD_DIM = 64
LANES = 128
HEADS_PER_TILE = LANES // HEAD_DIM
FORGET_ROWS = 16
EPS = 1e-6
LOG2E = 1.4426950408889634
QK_SCALE = HEAD_DIM ** -0.5 * LOG2E
NEG = -1e30
DEAD_LOG2 = 160.0
VMEM_LIMIT = 56 * 1024 * 1024

F32 = jnp.float32
BF16 = jnp.bfloat16
NT_DIMS = (((1,), (1,)), ((), ()))


def _rms_scale(x):
    return x * lax.rsqrt(jnp.mean(x * x, axis=-1, keepdims=True) + EPS)


def _const_spec(shape):
    return pl.BlockSpec(shape, lambda *_: (0,) * len(shape), pipeline_mode=pl.Buffered(1))


def _head_mask(rows, hh):
    lane = lax.broadcasted_iota(jnp.int32, (rows, LANES), 1)
    return (lane >= hh * HEAD_DIM) & (lane < (hh + 1) * HEAD_DIM)


def _proj_kernel(*refs, mix, tk, has_forget, n_aliased):
    refs = list(refs)
    x_ref, g_ref, wn_ref, wt_ref = refs[:4]
    del refs[:4]
    if has_forget:
        bf_ref = refs.pop(0)
    del refs[:n_aliased]
    q_ref, qm_ref, kt_ref, vt_ref, ktb_ref, vtb_ref = refs[:6]
    if has_forget:
        lft_ref = refs[6]
    tm = x_ref.shape[0]
    hn = (_rms_scale(x_ref[...]) * g_ref[...]).astype(BF16)
    qq = jnp.dot(hn, wn_ref[...], preferred_element_type=F32) * QK_SCALE
    q_ref[...] = qq[:, :mix].astype(BF16)
    qm_ref[...] = qq[:, mix:].astype(BF16)
    t = lax.dot_general(wt_ref[...], hn, NT_DIMS, preferred_element_type=F32)
    kt = t[:mix]
    vt = t[mix:2 * mix]
    kt_ref[0, 0] = kt
    vt_ref[0, 0] = vt
    for c in range(tm // tk):
        ktb_ref[0, c] = kt[:, c * tk:(c + 1) * tk].astype(BF16)
        vtb_ref[0, c] = vt[:, c * tk:(c + 1) * tk].astype(BF16)
    if has_forget:
        f = t[2 * mix:] + bf_ref[...]
        lft_ref[0] = jnp.minimum(f, 0.0) - jnp.log(1.0 + jnp.exp(-jnp.abs(f)))


def _project(x, g, wn, wt, bias, *, batch, mix, memw, tm, tk, slot=0, n_slots=1, stacked=None):
    rows, d = x.shape
    t_len = rows // batch
    per_batch = t_len // tm
    has_forget = bias is not None
    row = lambda width: pl.BlockSpec((tm, width), lambda r: (r, 0))
    feat = pl.BlockSpec((1, FORGET_ROWS, tm), lambda r: (r // per_batch, 0, r % per_batch))
    slotted = pl.BlockSpec((1, 1, mix, tm), lambda r: (slot, r // per_batch, 0, r % per_batch))
    blocked = pl.BlockSpec((1, tm // tk, mix, tk), lambda r: (r // per_batch, r % per_batch, 0, 0))
    in_specs = [row(d), _const_spec((1, d)), _const_spec(wn.shape), _const_spec(wt.shape)]
    args = [x, g.reshape(1, d), wn, wt]
    if has_forget:
        in_specs.append(_const_spec((FORGET_ROWS, 1)))
        args.append(bias)
    aliases = {}
    if stacked is not None:
        for a in stacked:
            aliases[len(args)] = 2 + len(aliases)
            in_specs.append(pl.BlockSpec(memory_space=pl.ANY))
            args.append(a)
    out_shape = [jax.ShapeDtypeStruct((rows, mix), BF16),
                 jax.ShapeDtypeStruct((rows, memw), BF16),
                 jax.ShapeDtypeStruct((n_slots, batch, mix, t_len), F32),
                 jax.ShapeDtypeStruct((n_slots, batch, mix, t_len), F32),
                 jax.ShapeDtypeStruct((batch, t_len // tk, mix, tk), BF16),
                 jax.ShapeDtypeStruct((batch, t_len // tk, mix, tk), BF16)]
    out_specs = [row(mix), row(memw), slotted, slotted, blocked, blocked]
    if has_forget:
        out_shape.append(jax.ShapeDtypeStruct((batch, FORGET_ROWS, t_len), F32))
        out_specs.append(feat)
    return pl.pallas_call(
        functools.partial(_proj_kernel, mix=mix, tk=tk, has_forget=has_forget,
                          n_aliased=len(aliases)),
        grid=(rows // tm,),
        in_specs=in_specs, out_specs=out_specs, out_shape=out_shape,
        input_output_aliases=aliases,
        compiler_params=pltpu.CompilerParams(
            dimension_semantics=("arbitrary",), vmem_limit_bytes=VMEM_LIMIT),
        name="proj_fox" if has_forget else "proj_sb",
    )(*args)


def _memkv_kernel(x_ref, g_ref, wt_ref, kt_ref, vt_ref, *, memw):
    hn = (_rms_scale(x_ref[0]) * g_ref[0]).astype(BF16)
    kvt = lax.dot_general(wt_ref[0], hn, NT_DIMS, preferred_element_type=F32)
    kt_ref[0, 0] = kvt[:memw]
    vt_ref[0, 0] = kvt[memw:]


def _memory_kv(mem, g_mem, wt_mem_kv):
    batch, n_mem, d = mem.shape
    depth, two_memw, _ = wt_mem_kv.shape
    memw = two_memw // 2
    out_spec = pl.BlockSpec((1, 1, memw, n_mem), lambda l, b: (l, b, 0, 0))
    return pl.pallas_call(
        functools.partial(_memkv_kernel, memw=memw),
        grid=(depth, batch),
        in_specs=[pl.BlockSpec((1, n_mem, d), lambda l, b: (b, 0, 0)),
                  pl.BlockSpec((1, 1, d), lambda l, b: (l, 0, 0)),
                  pl.BlockSpec((1, two_memw, d), lambda l, b: (l, 0, 0))],
        out_specs=[out_spec] * 2,
        out_shape=[jax.ShapeDtypeStruct((depth, batch, memw, n_mem), F32)] * 2,
        compiler_params=pltpu.CompilerParams(
            dimension_semantics=("arbitrary", "arbitrary"), vmem_limit_bytes=VMEM_LIMIT),
        name="memkv",
    )(mem, g_mem.reshape(depth, 1, d), wt_mem_kv)


def _split3(x):
    hi = x.astype(BF16)
    r1 = x - hi.astype(F32)
    mid = r1.astype(BF16)
    lo = (r1 - mid.astype(F32)).astype(BF16)
    return hi, mid, lo


def _cumsum_kernel(lf_ref, c_ref, *, t_len):
    r = lax.broadcasted_iota(jnp.int32, (LANES, LANES), 0)
    c = lax.broadcasted_iota(jnp.int32, (LANES, LANES), 1)
    upper = (r <= c).astype(BF16)
    carry = jnp.zeros((lf_ref.shape[1], 1), F32)
    for ch in range(t_len // LANES):
        cs = carry
        for part in _split3(lf_ref[0, :, ch * LANES:(ch + 1) * LANES]):
            cs = cs + jnp.dot(part, upper, preferred_element_type=F32)
        c_ref[0, :, ch * LANES:(ch + 1) * LANES] = cs * LOG2E
        carry = cs[:, LANES - 1:LANES]


def _cumulative_logf(lft):
    b, rows, t_len = lft.shape
    spec = pl.BlockSpec((1, rows, t_len), lambda i: (i, 0, 0))
    return pl.pallas_call(
        functools.partial(_cumsum_kernel, t_len=t_len),
        grid=(b,), in_specs=[spec], out_specs=spec,
        out_shape=jax.ShapeDtypeStruct((b, rows, t_len), F32),
        compiler_params=pltpu.CompilerParams(
            dimension_semantics=("arbitrary",), vmem_limit_bytes=VMEM_LIMIT),
        name="cumsum_logf",
    )(lft)


def _key_bias_layout(c, n_heads, tk):
    b, _, t_k = c.shape
    c = c[:, :n_heads].reshape(b, n_heads // HEADS_PER_TILE, HEADS_PER_TILE, t_k // tk, tk)
    c = jnp.transpose(c, (0, 1, 3, 2, 4))
    return jnp.pad(c, ((0, 0), (0, 0), (0, 0), (0, 8 - HEADS_PER_TILE), (0, 0)))


def _attn_kernel(*refs, mode, tq, tk, ks, past, cached):
    refs = list(refs)
    q_ref = refs.pop(0)
    if cached:
        ckt_ref, cvt_ref, nkt_ref, nvt_ref = refs[:4]
        refs = refs[4:]
    else:
        kt_ref, vt_ref = refs[:2]
        refs = refs[2:]
    c_ref = refs.pop(0) if mode == "fox" else None
    (o_ref,) = refs

    spb = ks // tk
    if cached:
        n_super = past // ks
        diag_blocks = [past // tk]

        def kv_block(j):
            if j == past // tk:
                return nkt_ref[0], nvt_ref[0]
            cols = slice(j * tk, (j + 1) * tk)
            return ckt_ref[0, 0, :, cols].astype(BF16), cvt_ref[0, 0, :, cols].astype(BF16)
    else:
        n_super = pl.program_id(2) * (tq // ks)
        diag_blocks = [n_super * spb + t for t in range(tq // tk)]

        def kv_block(j):
            return kt_ref[0, j], vt_ref[0, j]

    q = q_ref[0]
    qh = [jnp.where(_head_mask(tq, hh), q, jnp.zeros_like(q))
          for hh in range(HEADS_PER_TILE)]

    heads = range(HEADS_PER_TILE)
    groups = [(slice(d * tk, min((d + 1) * tk, tq)), diag_blocks[:d + 1], d * tk)
              for d in range(len(diag_blocks))]

    def take(x, rows):
        return x if rows == slice(None) else x[rows]

    def visible_mask(rows, width, koff, roff, strict):
        r = lax.broadcasted_iota(jnp.int32, (rows, width), 0) + roff
        c = lax.broadcasted_iota(jnp.int32, (rows, width), 1) + koff
        return c < r if strict else c <= r

    if mode == "fox":
        def scores(blocks, hh, rows=slice(None), roff=None):
            qr = take(qh[hh], rows)
            s = [jnp.dot(qr, kv_block(j)[0], preferred_element_type=F32) for j in blocks]
            cb = [c_ref[0, 0, j][hh:hh + 1, :] for j in blocks]
            s = (s[0] if len(s) == 1 else jnp.concatenate(s, axis=1))
            s = s - (cb[0] if len(cb) == 1 else jnp.concatenate(cb, axis=1))
            if roff is not None:
                s = jnp.where(visible_mask(s.shape[0], s.shape[1], 0, roff, strict=False), s, NEG)
            return s

        def softmax_step(s, m, l):
            m_new = jnp.maximum(m, jnp.max(s, axis=1, keepdims=True))
            alpha = jnp.exp2(m - m_new)
            p = jnp.exp2(s - m_new)
            lanes = p[:, :LANES]
            for t in range(1, s.shape[1] // LANES):
                lanes = lanes + p[:, t * LANES:(t + 1) * LANES]
            return m_new, alpha * l + lanes, alpha, p.astype(BF16)

        def weighted_values(acc, alpha, p, blocks):
            acc = alpha * acc
            for t, j in enumerate(blocks):
                acc = acc + lax.dot_general(p[:, t * tk:(t + 1) * tk], kv_block(j)[1], NT_DIMS,
                                            preferred_element_type=F32)
            return acc

        def step(carry, specs):
            m, l, acc = (list(carry[i::3]) for i in range(3))
            s = [[scores(blocks, hh, rows, roff) for hh in heads] for rows, blocks, roff in specs]
            new = [[None] * len(specs) for _ in heads]
            for g, (rows, blocks, _) in enumerate(specs):
                for hh in heads:
                    new[hh][g] = softmax_step(s[g][hh], take(m[hh], rows), take(l[hh], rows))
            out = []
            for hh in heads:
                parts = []
                for g, (rows, blocks, _) in enumerate(specs):
                    m_g, l_g, alpha, p = new[hh][g]
                    parts.append((m_g, l_g, weighted_values(take(acc[hh], rows), alpha, p, blocks)))
                out += [parts[0][i] if len(parts) == 1 else jnp.concatenate([x[i] for x in parts], axis=0)
                        for i in range(3)]
            return tuple(out)

        carry = (jnp.full((tq, 1), NEG, F32), jnp.zeros((tq, LANES), F32),
                 jnp.zeros((tq, LANES), F32)) * HEADS_PER_TILE
        whole = lambda blocks: [(slice(None), blocks, None)]
        if cached:
            if n_super:
                carry = step(carry, whole(list(range(n_super * spb))))
        else:
            carry = lax.fori_loop(
                0, n_super,
                lambda j, cy: step(cy, whole([j * spb + t for t in range(spb)])), carry)
        carry = step(carry, groups)
        head_out = [carry[3 * hh + 2] / jnp.sum(carry[3 * hh + 1], axis=1, keepdims=True)
                    for hh in heads]
    else:
        r = lax.broadcasted_iota(jnp.int32, (tk, tk), 0)
        c = lax.broadcasted_iota(jnp.int32, (tk, tk), 1)
        later = (r > c).astype(BF16)

        def update(state, blocks, rows=slice(None), roff=None):
            kv = [kv_block(j) for j in blocks]
            nrows = state[0].shape[0]
            visible = [None] * len(kv)
            if roff is not None:
                visible = [None if (t + 1) * tk <= roff
                           else visible_mask(nrows, tk, t * tk, roff, strict=True)
                           for t in range(len(kv))]
            out = []
            for hh in heads:
                run, acc = state[2 * hh], state[2 * hh + 1]
                qr = take(qh[hh], rows)
                logits, block_sums = [], []
                for t, (kt, _) in enumerate(kv):
                    z = jnp.dot(qr, kt, preferred_element_type=F32)
                    sp = jnp.maximum(z, 0.0) + jnp.log2(1.0 + jnp.exp2(-jnp.abs(z)))
                    if visible[t] is not None:
                        sp = jnp.where(visible[t], sp, 0.0)
                    rest = jnp.dot(sp.astype(BF16), later, preferred_element_type=F32)
                    logits.append(z - sp - rest)
                    block_sums.append(jnp.sum(sp, axis=1, keepdims=True))
                for t in reversed(range(len(kv))):
                    a = jnp.exp2(logits[t] - run)
                    if visible[t] is not None:
                        a = jnp.where(visible[t], a, 0.0)
                    acc = acc + lax.dot_general(a.astype(BF16), kv[t][1], NT_DIMS,
                                                preferred_element_type=F32)
                    run = run + block_sums[t]
                out += [run, acc]
            return tuple(out)

        def weights_vanish(state):
            run_min = jnp.minimum(jnp.min(state[0]), jnp.min(state[2]))
            return run_min > DEAD_LOG2

        state = (jnp.zeros((tq, 1), F32), jnp.zeros((tq, LANES), F32)) * HEADS_PER_TILE
        state = update(state, diag_blocks, roff=0)
        n_before = n_super * spb
        parts = []
        for rows, _, _ in groups:
            st = tuple(take(x, rows) for x in state)
            if cached:
                for j in reversed(range(n_before)):
                    st = lax.cond(weights_vanish(st), lambda s_: s_,
                                  lambda s_, j=j, rows=rows: update(s_, [j], rows), st)
            else:
                _, st = lax.while_loop(
                    lambda cy: (cy[0] < n_before) & jnp.logical_not(weights_vanish(cy[1])),
                    lambda cy, rows=rows: (cy[0] + 1, update(cy[1], [n_before - 1 - cy[0]], rows)),
                    (jnp.int32(0), st))
            parts.append(st)
        state = tuple(parts[0][i] if len(parts) == 1 else jnp.concatenate([x[i] for x in parts], axis=0)
                      for i in range(len(state)))
        head_out = [state[2 * hh + 1] for hh in heads]

    o_ref[0] = jnp.where(_head_mask(tq, 0), head_out[0], head_out[1]).astype(o_ref.dtype)


def _attention(q, kv, c, *, mode, tq, tk, ks, past=0, cache_slot=0):
    b, t_q, width = q.shape
    cached = len(kv) == 4
    pairs = width // LANES
    q_spec = pl.BlockSpec((1, tq, LANES), lambda bi, hp, qi: (bi, qi, hp))
    if cached:
        assert t_q == tq and tq <= tk and past % ks == 0 and ks % tk == 0
        old = pl.BlockSpec((1, 1, LANES, past), lambda bi, hp, qi: (cache_slot, bi, hp, 0))
        new = pl.BlockSpec((1, LANES, tk), lambda bi, hp, qi: (bi, hp, 0))
        kv_specs = [old, old, new, new]
    else:
        assert t_q % tq == 0 and tq % ks == 0 and ks % tk == 0 and past == 0
        blocked = pl.BlockSpec((1, t_q // tk, LANES, tk), lambda bi, hp, qi: (bi, 0, hp, 0))
        kv_specs = [blocked, blocked]
    in_specs = [q_spec] + kv_specs
    args = [q, *kv]
    if mode == "fox":
        in_specs.append(pl.BlockSpec((1, 1) + c.shape[2:], lambda bi, hp, qi: (bi, hp, 0, 0, 0)))
        args.append(c)
    return pl.pallas_call(
        functools.partial(_attn_kernel, mode=mode, tq=tq, tk=tk, ks=ks, past=past, cached=cached),
        grid=(b, pairs, t_q // tq),
        in_specs=in_specs,
        out_specs=q_spec,
        out_shape=jax.ShapeDtypeStruct((b, t_q, width), BF16),
        compiler_params=pltpu.CompilerParams(
            dimension_semantics=("arbitrary", "arbitrary", "arbitrary"),
            vmem_limit_bytes=VMEM_LIMIT),
        name=f"attn_{mode}",
    )(*args)


def _post_kernel(*refs, nb, mix, final):
    if final:
        (x_ref, o_ref, qm_ref, mkt_ref, mvt_ref, wo_ref, g_ref, wgu_ref, wd_ref, gf_ref,
         y_ref, acc_ref) = refs
    else:
        (x_ref, o_ref, qm_ref, mkt_ref, mvt_ref, wo_ref, g_ref, wgu_ref, wd_ref,
         xo_ref, acc_ref) = refs
    tm = x_ref.shape[0]
    rb = tm // nb
    memw = qm_ref.shape[1]

    om_rows = []
    for bi in range(nb):
        tiles = []
        for tl in range(memw // LANES):
            feats = slice(tl * LANES, (tl + 1) * LANES)
            qp = qm_ref[bi * rb:(bi + 1) * rb, feats]
            mkt = mkt_ref[0, bi, feats, :].astype(BF16)
            mvt = mvt_ref[0, bi, feats, :].astype(BF16)
            tile = None
            for hh in range(HEADS_PER_TILE):
                in_head = _head_mask(rb, hh)
                qh = jnp.where(in_head, qp, jnp.zeros_like(qp))
                s = jnp.dot(qh, mkt, preferred_element_type=F32)
                p = jnp.exp2(s - jnp.max(s, axis=1, keepdims=True))
                o = lax.dot_general(p.astype(BF16), mvt, NT_DIMS, preferred_element_type=F32)
                o = o / jnp.sum(p, axis=1, keepdims=True)
                tile = o if tile is None else jnp.where(in_head, o, tile)
            tiles.append(tile)
        om_rows.append(jnp.concatenate(tiles, axis=1))
    om = (om_rows[0] if nb == 1 else jnp.concatenate(om_rows, axis=0)).astype(BF16)

    x1 = (x_ref[...]
          + jnp.dot(o_ref[...], wo_ref[0:mix, :], preferred_element_type=F32)
          + jnp.dot(om, wo_ref[mix:mix + memw, :], preferred_element_type=F32))
    h = (_rms_scale(x1) * g_ref[...]).astype(BF16)
    ffc = wd_ref.shape[1]

    acc_ref[...] = x1

    def chunk(ci, _):
        gu = jnp.dot(h, wgu_ref[ci], preferred_element_type=F32)
        gate = gu[:, :ffc]
        act = (gate * jax.nn.sigmoid(gate) * gu[:, ffc:]).astype(BF16)
        acc_ref[...] += jnp.dot(act, wd_ref[ci], preferred_element_type=F32)
        return 0

    lax.fori_loop(0, wgu_ref.shape[0], chunk, 0)
    if final:
        y_ref[...] = _rms_scale(acc_ref[...]) * gf_ref[...]
    else:
        xo_ref[...] = acc_ref[...]


def _post(x, o_mix, qm, mkt, mvt, layer, w_out, g_ffn, w_gu, w_down, g_final, *, tm, nb):
    rows, d = x.shape
    mix = o_mix.shape[1]
    memw = qm.shape[1]
    n_mem = mkt.shape[3]
    final = g_final is not None
    rows_per_batch = rows // mkt.shape[1]
    assert tm == nb * rows_per_batch or (nb == 1 and rows_per_batch % tm == 0)
    blocks_per_batch = max(rows_per_batch // tm, 1)
    row = lambda width: pl.BlockSpec((tm, width), lambda r: (r, 0))
    mem_spec = pl.BlockSpec((1, nb, memw, n_mem), lambda r: (layer, r // blocks_per_batch, 0, 0))
    in_specs = [row(d), row(mix), row(memw), mem_spec, mem_spec,
                _const_spec(w_out.shape), _const_spec((1, d)),
                _const_spec(w_gu.shape), _const_spec(w_down.shape)]
    args = [x, o_mix, qm, mkt, mvt, w_out, g_ffn.reshape(1, d), w_gu, w_down]
    if final:
        in_specs.append(_const_spec((1, d)))
        args.append(g_final.reshape(1, d))
    return pl.pallas_call(
        functools.partial(_post_kernel, nb=nb, mix=mix, final=final),
        grid=(rows // tm,),
        in_specs=in_specs,
        out_specs=row(d),
        out_shape=jax.ShapeDtypeStruct((rows, d), F32),
        scratch_shapes=[pltpu.VMEM((tm, d), F32)],
        compiler_params=pltpu.CompilerParams(
            dimension_semantics=("arbitrary",), vmem_limit_bytes=VMEM_LIMIT),
        name="post_final" if final else "post",
    )(*args)


def _prep_w_in(w, mix, n_forget):
    wn = jnp.concatenate([w[:, :mix], w[:, 3 * mix + n_forget:]], axis=1)
    wt = w[:, mix:3 * mix + n_forget]
    if n_forget:
        wt = jnp.pad(wt, ((0, 0), (0, FORGET_ROWS - n_forget)))
    return wn.astype(BF16), wt.T.astype(BF16)


def _ff_chunk(d_ff):
    for c in (2816, 1408, 512, 256, 128):
        if d_ff % c == 0:
            return c
    raise ValueError(f"d_ff={d_ff} is not a multiple of {LANES}")


def _prep_ffn(w_gate_up, w_down):
    d, two_ff = w_gate_up.shape
    d_ff = two_ff // 2
    ffc = _ff_chunk(d_ff)
    n = d_ff // ffc
    gate = w_gate_up[:, :d_ff].reshape(d, n, ffc)
    up = w_gate_up[:, d_ff:].reshape(d, n, ffc)
    w_gu = jnp.transpose(jnp.concatenate([gate, up], axis=2), (1, 0, 2)).astype(BF16)
    return w_gu, w_down.reshape(n, ffc, d).astype(BF16)


def _feature_major(cache):
    *lead, t, h, hd = cache.shape
    return jnp.moveaxis(cache, -3, -1).reshape(*lead, h * hd, t)


def _token_major(xt):
    *lead, width, t = xt.shape
    return jnp.moveaxis(xt.reshape(*lead, width // HEAD_DIM, HEAD_DIM, t), -1, -3)


def kernel(x_prompt, x_sample, mem_prompt, cache_fox_k, cache_fox_v, cache_fox_logf, cache_sb_k, cache_sb_v, cache_mem_k, cache_mem_v, g_mix, w_in_fox, b_f, w_in_sb, g_mem, w_mem_kv, w_out, g_ffn, w_gate_up, w_down, g_final):
    bp, seq, d = x_prompt.shape
    bs, dec_seq, _ = x_sample.shape
    n_mem = mem_prompt.shape[1]
    depth = g_mix.shape[0]
    n_heads = b_f.shape[1]
    mix = n_heads * HEAD_DIM
    memw = w_mem_kv.shape[2] // 2
    past = cache_fox_k.shape[2]

    tk = 256
    ks = 512
    tq_p = ks
    tm_p = 512
    rows_s = bs * dec_seq
    assert seq % tq_p == 0 and seq % tm_p == 0 and rows_s % tk == 0
    assert dec_seq <= tk and past % ks == 0

    xp = x_prompt.reshape(bp * seq, d)
    xs = x_sample.reshape(rows_s, d)

    mkt_p, mvt_p = _memory_kv(mem_prompt, g_mem, jnp.transpose(w_mem_kv, (0, 2, 1)).astype(BF16))

    def new_keys(xt):
        xt = jnp.transpose(xt.reshape(mix, bs, dec_seq), (1, 0, 2))
        return jnp.pad(xt, ((0, 0), (0, 0), (0, tk - dec_seq))).astype(BF16)

    caches = {"fox": (_feature_major(cache_fox_k), _feature_major(cache_fox_v)),
              "sb": (_feature_major(cache_sb_k), _feature_major(cache_sb_v))}
    cmkt, cmvt = _feature_major(cache_mem_k), _feature_major(cache_mem_v)
    n_layers = {"fox": (depth + 1) // 2, "sb": depth // 2}
    kv_p = {"fox": None, "sb": None}
    fl_p = []
    fk_s, fv_s, fl_s, sk_s, sv_s = [], [], [], [], []
    for i in range(depth):
        j = i // 2
        is_fox = i % 2 == 0
        mode = "fox" if is_fox else "sb"
        if is_fox:
            wn, wt = _prep_w_in(w_in_fox[j], mix, n_heads)
            bias = jnp.pad(b_f[j], (0, FORGET_ROWS - n_heads)).reshape(FORGET_ROWS, 1)
        else:
            wn, wt = _prep_w_in(w_in_sb[j], mix, 0)
            bias = None
        w_gu, w_dn = _prep_ffn(w_gate_up[i], w_down[i])
        w_o = w_out[i].astype(BF16)
        g_fin = g_final if i == depth - 1 else None

        outs = _project(xp, g_mix[i], wn, wt, bias, batch=bp, mix=mix, memw=memw, tm=tm_p, tk=tk,
                        slot=j, n_slots=n_layers[mode], stacked=kv_p[mode])
        q, qm, kt, vt, ktb, vtb = outs[:6]
        kv_p[mode] = (kt, vt)
        c = None
        if is_fox:
            lft = outs[6]
            c = _key_bias_layout(_cumulative_logf(lft), n_heads, tk)
            fl_p.append(jnp.transpose(lft[:, :n_heads], (0, 2, 1)))
        o = _attention(q.reshape(bp, seq, mix), (ktb, vtb), c, mode=mode, tq=tq_p, tk=tk, ks=ks)
        xp_new = _post(xp, o.reshape(bp * seq, mix), qm, mkt_p, mvt_p, i,
                       w_o, g_ffn[i], w_gu, w_dn, g_fin, tm=tm_p, nb=1)

        outs = _project(xs, g_mix[i], wn, wt, bias, batch=1, mix=mix, memw=memw, tm=rows_s, tk=tk)
        q, qm, kt, vt = outs[:4]
        c = None
        if is_fox:
            lft = outs[6]
            lft_s = jnp.transpose(lft.reshape(FORGET_ROWS, bs, dec_seq), (1, 0, 2))
            lft_past = jnp.pad(jnp.transpose(cache_fox_logf[j], (0, 2, 1)),
                               ((0, 0), (0, FORGET_ROWS - n_heads), (0, 0)))
            lft_all = jnp.concatenate(
                [lft_past, jnp.pad(lft_s, ((0, 0), (0, 0), (0, tk - dec_seq)))], axis=2)
            c = _key_bias_layout(_cumulative_logf(lft_all), n_heads, tk)
            fl_s.append(jnp.transpose(lft_s[:, :n_heads], (0, 2, 1)))
        new_shape = (n_heads, HEAD_DIM, bs, dec_seq)
        k_new = jnp.transpose(kt.reshape(new_shape), (2, 3, 0, 1))
        v_new = jnp.transpose(vt.reshape(new_shape), (2, 3, 0, 1))
        if is_fox:
            fk_s.append(k_new)
            fv_s.append(v_new)
        else:
            sk_s.append(k_new)
            sv_s.append(v_new)
        o = _attention(q.reshape(bs, dec_seq, mix),
                       (*caches[mode], new_keys(kt), new_keys(vt)), c,
                       mode=mode, tq=dec_seq, tk=tk, ks=ks, past=past, cache_slot=j)
        xs_new = _post(xs, o.reshape(rows_s, mix), qm, cmkt, cmvt, i,
                       w_o, g_ffn[i], w_gu, w_dn, g_fin, tm=rows_s, nb=bs)
        xp, xs = xp_new, xs_new

    return (xp.reshape(bp, seq, d), xs.reshape(bs, dec_seq, d),
            _token_major(kv_p["fox"][0]), _token_major(kv_p["fox"][1]), jnp.stack(fl_p),
            _token_major(kv_p["sb"][0]), _token_major(kv_p["sb"][1]),
            _token_major(mkt_p), _token_major(mvt_p),
            jnp.stack(fk_s), jnp.stack(fv_s), jnp.stack(fl_s), jnp.stack(sk_s), jnp.stack(sv_s))
```

```python
import functools

import jax
import jax.numpy as jnp
from jax import lax
from jax.experimental import pallas as pl
from jax.experimental.pallas import tpu as pltpu

HEAD_DIM = 64
LANES = 128
HEADS_PER_TILE = LANES // HEAD_DIM
FORGET_ROWS = 16
EPS = 1e-6
LOG2E = 1.4426950408889634
QK_SCALE = HEAD_DIM ** -0.5 * LOG2E
NEG = -1e30
DEAD_LOG2 = 160.0
VMEM_LIMIT = 56 * 1024 * 1024

F32 = jnp.float32
BF16 = jnp.bfloat16
NT_DIMS = (((1,), (1,)), ((), ()))


def _rms_scale(x):
    return x * lax.rsqrt(jnp.mean(x * x, axis=-1, keepdims=True) + EPS)


def _const_spec(shape):
    return pl.BlockSpec(shape, lambda *_: (0,) * len(shape), pipeline_mode=pl.Buffered(1))


def _head_mask(rows, hh):
    lane = lax.broadcasted_iota(jnp.int32, (rows, LANES), 1)
    return (lane >= hh * HEAD_DIM) & (lane < (hh + 1) * HEAD_DIM)


def _proj_kernel(*refs, mix, tk, has_forget, n_aliased):
    refs = list(refs)
    x_ref, g_ref, wn_ref, wt_ref = refs[:4]
    del refs[:4]
    if has_forget:
        bf_ref = refs.pop(0)
    del refs[:n_aliased]
    q_ref, qm_ref, kt_ref, vt_ref, ktb_ref, vtb_ref = refs[:6]
    if has_forget:
        lft_ref = refs[6]
    tm = x_ref.shape[0]
    hn = (_rms_scale(x_ref[...]) * g_ref[...]).astype(BF16)
    qq = jnp.dot(hn, wn_ref[...], preferred_element_type=F32) * QK_SCALE
    q_ref[...] = qq[:, :mix].astype(BF16)
    qm_ref[...] = qq[:, mix:].astype(BF16)
    t = lax.dot_general(wt_ref[...], hn, NT_DIMS, preferred_element_type=F32)
    kt = t[:mix]
    vt = t[mix:2 * mix]
    kt_ref[0, 0] = kt
    vt_ref[0, 0] = vt
    for c in range(tm // tk):
        ktb_ref[0, c] = kt[:, c * tk:(c + 1) * tk].astype(BF16)
        vtb_ref[0, c] = vt[:, c * tk:(c + 1) * tk].astype(BF16)
    if has_forget:
        f = t[2 * mix:] + bf_ref[...]
        lft_ref[0] = jnp.minimum(f, 0.0) - jnp.log(1.0 + jnp.exp(-jnp.abs(f)))


def _project(x, g, wn, wt, bias, *, batch, mix, memw, tm, tk, slot=0, n_slots=1, stacked=None):
    rows, d = x.shape
    t_len = rows // batch
    per_batch = t_len // tm
    has_forget = bias is not None
    row = lambda width: pl.BlockSpec((tm, width), lambda r: (r, 0))
    feat = pl.BlockSpec((1, FORGET_ROWS, tm), lambda r: (r // per_batch, 0, r % per_batch))
    slotted = pl.BlockSpec((1, 1, mix, tm), lambda r: (slot, r // per_batch, 0, r % per_batch))
    blocked = pl.BlockSpec((1, tm // tk, mix, tk), lambda r: (r // per_batch, r % per_batch, 0, 0))
    in_specs = [row(d), _const_spec((1, d)), _const_spec(wn.shape), _const_spec(wt.shape)]
    args = [x, g.reshape(1, d), wn, wt]
    if has_forget:
        in_specs.append(_const_spec((FORGET_ROWS, 1)))
        args.append(bias)
    aliases = {}
    if stacked is not None:
        for a in stacked:
            aliases[len(args)] = 2 + len(aliases)
            in_specs.append(pl.BlockSpec(memory_space=pl.ANY))
            args.append(a)
    out_shape = [jax.ShapeDtypeStruct((rows, mix), BF16),
                 jax.ShapeDtypeStruct((rows, memw), BF16),
                 jax.ShapeDtypeStruct((n_slots, batch, mix, t_len), F32),
                 jax.ShapeDtypeStruct((n_slots, batch, mix, t_len), F32),
                 jax.ShapeDtypeStruct((batch, t_len // tk, mix, tk), BF16),
                 jax.ShapeDtypeStruct((batch, t_len // tk, mix, tk), BF16)]
    out_specs = [row(mix), row(memw), slotted, slotted, blocked, blocked]
    if has_forget:
        out_shape.append(jax.ShapeDtypeStruct((batch, FORGET_ROWS, t_len), F32))
        out_specs.append(feat)
    return pl.pallas_call(
        functools.partial(_proj_kernel, mix=mix, tk=tk, has_forget=has_forget,
                          n_aliased=len(aliases)),
        grid=(rows // tm,),
        in_specs=in_specs, out_specs=out_specs, out_shape=out_shape,
        input_output_aliases=aliases,
        compiler_params=pltpu.CompilerParams(
            dimension_semantics=("arbitrary",), vmem_limit_bytes=VMEM_LIMIT),
        name="proj_fox" if has_forget else "proj_sb",
    )(*args)


def _memkv_kernel(x_ref, g_ref, wt_ref, kt_ref, vt_ref, *, memw):
    hn = (_rms_scale(x_ref[0]) * g_ref[0]).astype(BF16)
    kvt = lax.dot_general(wt_ref[0], hn, NT_DIMS, preferred_element_type=F32)
    kt_ref[0, 0] = kvt[:memw]
    vt_ref[0, 0] = kvt[memw:]


def _memory_kv(mem, g_mem, wt_mem_kv):
    batch, n_mem, d = mem.shape
    depth, two_memw, _ = wt_mem_kv.shape
    memw = two_memw // 2
    out_spec = pl.BlockSpec((1, 1, memw, n_mem), lambda l, b: (l, b, 0, 0))
    return pl.pallas_call(
        functools.partial(_memkv_kernel, memw=memw),
        grid=(depth, batch),
        in_specs=[pl.BlockSpec((1, n_mem, d), lambda l, b: (b, 0, 0)),
                  pl.BlockSpec((1, 1, d), lambda l, b: (l, 0, 0)),
                  pl.BlockSpec((1, two_memw, d), lambda l, b: (l, 0, 0))],
        out_specs=[out_spec] * 2,
        out_shape=[jax.ShapeDtypeStruct((depth, batch, memw, n_mem), F32)] * 2,
        compiler_params=pltpu.CompilerParams(
            dimension_semantics=("arbitrary", "arbitrary"), vmem_limit_bytes=VMEM_LIMIT),
        name="memkv",
    )(mem, g_mem.reshape(depth, 1, d), wt_mem_kv)


def _split3(x):
    hi = x.astype(BF16)
    r1 = x - hi.astype(F32)
    mid = r1.astype(BF16)
    lo = (r1 - mid.astype(F32)).astype(BF16)
    return hi, mid, lo


def _cumsum_kernel(lf_ref, c_ref, *, t_len):
    r = lax.broadcasted_iota(jnp.int32, (LANES, LANES), 0)
    c = lax.broadcasted_iota(jnp.int32, (LANES, LANES), 1)
    upper = (r <= c).astype(BF16)
    carry = jnp.zeros((lf_ref.shape[1], 1), F32)
    for ch in range(t_len // LANES):
        cs = carry
        for part in _split3(lf_ref[0, :, ch * LANES:(ch + 1) * LANES]):
            cs = cs + jnp.dot(part, upper, preferred_element_type=F32)
        c_ref[0, :, ch * LANES:(ch + 1) * LANES] = cs * LOG2E
        carry = cs[:, LANES - 1:LANES]


def _cumulative_logf(lft):
    b, rows, t_len = lft.shape
    spec = pl.BlockSpec((1, rows, t_len), lambda i: (i, 0, 0))
    return pl.pallas_call(
        functools.partial(_cumsum_kernel, t_len=t_len),
        grid=(b,), in_specs=[spec], out_specs=spec,
        out_shape=jax.ShapeDtypeStruct((b, rows, t_len), F32),
        compiler_params=pltpu.CompilerParams(
            dimension_semantics=("arbitrary",), vmem_limit_bytes=VMEM_LIMIT),
        name="cumsum_logf",
    )(lft)


def _key_bias_layout(c, n_heads, tk):
    b, _, t_k = c.shape
    c = c[:, :n_heads].reshape(b, n_heads // HEADS_PER_TILE, HEADS_PER_TILE, t_k // tk, tk)
    c = jnp.transpose(c, (0, 1, 3, 2, 4))
    return jnp.pad(c, ((0, 0), (0, 0), (0, 0), (0, 8 - HEADS_PER_TILE), (0, 0)))


def _attn_kernel(*refs, mode, tq, tk, ks, past, cached):
    refs = list(refs)
    q_ref = refs.pop(0)
    if cached:
        ckt_ref, cvt_ref, nkt_ref, nvt_ref = refs[:4]
        refs = refs[4:]
    else:
        kt_ref, vt_ref = refs[:2]
        refs = refs[2:]
    c_ref = refs.pop(0) if mode == "fox" else None
    (o_ref,) = refs

    spb = ks // tk
    if cached:
        n_super = past // ks
        diag_blocks = [past // tk]

        def kv_block(j):
            if j == past // tk:
                return nkt_ref[0], nvt_ref[0]
            cols = slice(j * tk, (j + 1) * tk)
            return ckt_ref[0, 0, :, cols].astype(BF16), cvt_ref[0, 0, :, cols].astype(BF16)
    else:
        n_super = pl.program_id(2) * (tq // ks)
        diag_blocks = [n_super * spb + t for t in range(tq // tk)]

        def kv_block(j):
            return kt_ref[0, j], vt_ref[0, j]

    q = q_ref[0]
    qh = [jnp.where(_head_mask(tq, hh), q, jnp.zeros_like(q))
          for hh in range(HEADS_PER_TILE)]

    heads = range(HEADS_PER_TILE)
    groups = [(slice(d * tk, min((d + 1) * tk, tq)), diag_blocks[:d + 1], d * tk)
              for d in range(len(diag_blocks))]

    def take(x, rows):
        return x if rows == slice(None) else x[rows]

    def visible_mask(rows, width, koff, roff, strict):
        r = lax.broadcasted_iota(jnp.int32, (rows, width), 0) + roff
        c = lax.broadcasted_iota(jnp.int32, (rows, width), 1) + koff
        return c < r if strict else c <= r

    if mode == "fox":
        def scores(blocks, hh, rows=slice(None), roff=None):
            qr = take(qh[hh], rows)
            s = [jnp.dot(qr, kv_block(j)[0], preferred_element_type=F32) for j in blocks]
            cb = [c_ref[0, 0, j][hh:hh + 1, :] for j in blocks]
            s = (s[0] if len(s) == 1 else jnp.concatenate(s, axis=1))
            s = s - (cb[0] if len(cb) == 1 else jnp.concatenate(cb, axis=1))
            if roff is not None:
                s = jnp.where(visible_mask(s.shape[0], s.shape[1], 0, roff, strict=False), s, NEG)
            return s

        def softmax_step(s, m, l):
            m_new = jnp.maximum(m, jnp.max(s, axis=1, keepdims=True))
            alpha = jnp.exp2(m - m_new)
            p = jnp.exp2(s - m_new)
            lanes = p[:, :LANES]
            for t in range(1, s.shape[1] // LANES):
                lanes = lanes + p[:, t * LANES:(t + 1) * LANES]
            return m_new, alpha * l + lanes, alpha, p.astype(BF16)

        def weighted_values(acc, alpha, p, blocks):
            pv = None
            for t, j in enumerate(blocks):
                d = lax.dot_general(p[:, t * tk:(t + 1) * tk], kv_block(j)[1], NT_DIMS,
                                    preferred_element_type=F32)
                pv = d if pv is None else pv + d
            return alpha * acc + pv

        def step(carry, specs):
            m, l, acc = (list(carry[i::3]) for i in range(3))
            s = [[scores(blocks, hh, rows, roff) for hh in heads] for rows, blocks, roff in specs]
            new = [[None] * len(specs) for _ in heads]
            for g, (rows, blocks, _) in enumerate(specs):
                for hh in heads:
                    new[hh][g] = softmax_step(s[g][hh], take(m[hh], rows), take(l[hh], rows))
            out = []
            for hh in heads:
                parts = []
                for g, (rows, blocks, _) in enumerate(specs):
                    m_g, l_g, alpha, p = new[hh][g]
                    parts.append((m_g, l_g, weighted_values(take(acc[hh], rows), alpha, p, blocks)))
                out += [parts[0][i] if len(parts) == 1 else jnp.concatenate([x[i] for x in parts], axis=0)
                        for i in range(3)]
            return tuple(out)

        carry = (jnp.full((tq, 1), NEG, F32), jnp.zeros((tq, LANES), F32),
                 jnp.zeros((tq, LANES), F32)) * HEADS_PER_TILE
        whole = lambda blocks: [(slice(None), blocks, None)]
        if cached:
            if n_super:
                carry = step(carry, whole(list(range(n_super * spb))))
        else:
            carry = lax.fori_loop(
                0, n_super,
                lambda j, cy: step(cy, whole([j * spb + t for t in range(spb)])), carry)
        carry = step(carry, groups)
        head_out = [carry[3 * hh + 2] / jnp.sum(carry[3 * hh + 1], axis=1, keepdims=True)
                    for hh in heads]
    else:
        r = lax.broadcasted_iota(jnp.int32, (tk, tk), 0)
        c = lax.broadcasted_iota(jnp.int32, (tk, tk), 1)
        later = (r > c).astype(BF16)

        def update(state, blocks, rows=slice(None), roff=None):
            kv = [kv_block(j) for j in blocks]
            nrows = state[0].shape[0]
            visible = [None] * len(kv)
            if roff is not None:
                visible = [None if (t + 1) * tk <= roff
                           else visible_mask(nrows, tk, t * tk, roff, strict=True)
                           for t in range(len(kv))]
            out = []
            for hh in heads:
                run, acc = state[2 * hh], state[2 * hh + 1]
                qr = take(qh[hh], rows)
                logits, block_sums = [], []
                for t, (kt, _) in enumerate(kv):
                    z = jnp.dot(qr, kt, preferred_element_type=F32)
                    sp = jnp.maximum(z, 0.0) + jnp.log2(1.0 + jnp.exp2(-jnp.abs(z)))
                    if visible[t] is not None:
                        sp = jnp.where(visible[t], sp, 0.0)
                    rest = jnp.dot(sp.astype(BF16), later, preferred_element_type=F32)
                    logits.append(z - sp - rest)
                    block_sums.append(jnp.sum(sp, axis=1, keepdims=True))
                for t in reversed(range(len(kv))):
                    a = jnp.exp2(logits[t] - run)
                    if visible[t] is not None:
                        a = jnp.where(visible[t], a, 0.0)
                    acc = acc + lax.dot_general(a.astype(BF16), kv[t][1], NT_DIMS,
                                                preferred_element_type=F32)
                    run = run + block_sums[t]
                out += [run, acc]
            return tuple(out)

        def weights_vanish(state):
            run_min = jnp.minimum(jnp.min(state[0]), jnp.min(state[2]))
            return run_min > DEAD_LOG2

        state = (jnp.zeros((tq, 1), F32), jnp.zeros((tq, LANES), F32)) * HEADS_PER_TILE
        state = update(state, diag_blocks, roff=0)
        n_before = n_super * spb
        parts = []
        for rows, _, _ in groups:
            st = tuple(take(x, rows) for x in state)
            if cached:
                for j in reversed(range(n_before)):
                    st = lax.cond(weights_vanish(st), lambda s_: s_,
                                  lambda s_, j=j, rows=rows: update(s_, [j], rows), st)
            else:
                _, st = lax.while_loop(
                    lambda cy: (cy[0] < n_before) & jnp.logical_not(weights_vanish(cy[1])),
                    lambda cy, rows=rows: (cy[0] + 1, update(cy[1], [n_before - 1 - cy[0]], rows)),
                    (jnp.int32(0), st))
            parts.append(st)
        state = tuple(parts[0][i] if len(parts) == 1 else jnp.concatenate([x[i] for x in parts], axis=0)
                      for i in range(len(state)))
        head_out = [state[2 * hh + 1] for hh in heads]

    o_ref[0] = jnp.where(_head_mask(tq, 0), head_out[0], head_out[1]).astype(o_ref.dtype)


def _attention(q, kv, c, *, mode, tq, tk, ks, past=0, cache_slot=0):
    b, t_q, width = q.shape
    cached = len(kv) == 4
    pairs = width // LANES
    q_spec = pl.BlockSpec((1, tq, LANES), lambda bi, hp, qi: (bi, qi, hp))
    if cached:
        assert t_q == tq and tq <= tk and past % ks == 0 and ks % tk == 0
        old = pl.BlockSpec((1, 1, LANES, past), lambda bi, hp, qi: (cache_slot, bi, hp, 0))
        new = pl.BlockSpec((1, LANES, tk), lambda bi, hp, qi: (bi, hp, 0))
        kv_specs = [old, old, new, new]
    else:
        assert t_q % tq == 0 and tq % ks == 0 and ks % tk == 0 and past == 0
        blocked = pl.BlockSpec((1, t_q // tk, LANES, tk), lambda bi, hp, qi: (bi, 0, hp, 0))
        kv_specs = [blocked, blocked]
    in_specs = [q_spec] + kv_specs
    args = [q, *kv]
    if mode == "fox":
        in_specs.append(pl.BlockSpec((1, 1) + c.shape[2:], lambda bi, hp, qi: (bi, hp, 0, 0, 0)))
        args.append(c)
    return pl.pallas_call(
        functools.partial(_attn_kernel, mode=mode, tq=tq, tk=tk, ks=ks, past=past, cached=cached),
        grid=(b, pairs, t_q // tq),
        in_specs=in_specs,
        out_specs=q_spec,
        out_shape=jax.ShapeDtypeStruct((b, t_q, width), BF16),
        compiler_params=pltpu.CompilerParams(
            dimension_semantics=("arbitrary", "arbitrary", "arbitrary"),
            vmem_limit_bytes=VMEM_LIMIT),
        name=f"attn_{mode}",
    )(*args)


def _post_kernel(*refs, nb, mix, final):
    if final:
        (x_ref, o_ref, qm_ref, mkt_ref, mvt_ref, wo_ref, g_ref, wgu_ref, wd_ref, gf_ref,
         y_ref, acc_ref) = refs
    else:
        (x_ref, o_ref, qm_ref, mkt_ref, mvt_ref, wo_ref, g_ref, wgu_ref, wd_ref,
         xo_ref, acc_ref) = refs
    tm = x_ref.shape[0]
    rb = tm // nb
    memw = qm_ref.shape[1]

    om_rows = []
    for bi in range(nb):
        tiles = []
        for tl in range(memw // LANES):
            feats = slice(tl * LANES, (tl + 1) * LANES)
            qp = qm_ref[bi * rb:(bi + 1) * rb, feats]
            mkt = mkt_ref[0, bi, feats, :].astype(BF16)
            mvt = mvt_ref[0, bi, feats, :].astype(BF16)
            tile = None
            for hh in range(HEADS_PER_TILE):
                in_head = _head_mask(rb, hh)
                qh = jnp.where(in_head, qp, jnp.zeros_like(qp))
                s = jnp.dot(qh, mkt, preferred_element_type=F32)
                p = jnp.exp2(s - jnp.max(s, axis=1, keepdims=True))
                o = lax.dot_general(p.astype(BF16), mvt, NT_DIMS, preferred_element_type=F32)
                o = o / jnp.sum(p, axis=1, keepdims=True)
                tile = o if tile is None else jnp.where(in_head, o, tile)
            tiles.append(tile)
        om_rows.append(jnp.concatenate(tiles, axis=1))
    om = (om_rows[0] if nb == 1 else jnp.concatenate(om_rows, axis=0)).astype(BF16)

    x1 = (x_ref[...]
          + jnp.dot(o_ref[...], wo_ref[0:mix, :], preferred_element_type=F32)
          + jnp.dot(om, wo_ref[mix:mix + memw, :], preferred_element_type=F32))
    h = (_rms_scale(x1) * g_ref[...]).astype(BF16)
    ffc = wd_ref.shape[1]

    acc_ref[...] = x1

    def chunk(ci, _):
        gu = jnp.dot(h, wgu_ref[ci], preferred_element_type=F32)
        gate = gu[:, :ffc]
        act = (gate * jax.nn.sigmoid(gate) * gu[:, ffc:]).astype(BF16)
        acc_ref[...] += jnp.dot(act, wd_ref[ci], preferred_element_type=F32)
        return 0

    lax.fori_loop(0, wgu_ref.shape[0], chunk, 0)
    if final:
        y_ref[...] = _rms_scale(acc_ref[...]) * gf_ref[...]
    else:
        xo_ref[...] = acc_ref[...]


def _post(x, o_mix, qm, mkt, mvt, layer, w_out, g_ffn, w_gu, w_down, g_final, *, tm, nb):
    rows, d = x.shape
    mix = o_mix.shape[1]
    memw = qm.shape[1]
    n_mem = mkt.shape[3]
    final = g_final is not None
    rows_per_batch = rows // mkt.shape[1]
    assert tm == nb * rows_per_batch or (nb == 1 and rows_per_batch % tm == 0)
    blocks_per_batch = max(rows_per_batch // tm, 1)
    row = lambda width: pl.BlockSpec((tm, width), lambda r: (r, 0))
    mem_spec = pl.BlockSpec((1, nb, memw, n_mem), lambda r: (layer, r // blocks_per_batch, 0, 0))
    in_specs = [row(d), row(mix), row(memw), mem_spec, mem_spec,
                _const_spec(w_out.shape), _const_spec((1, d)),
                _const_spec(w_gu.shape), _const_spec(w_down.shape)]
    args = [x, o_mix, qm, mkt, mvt, w_out, g_ffn.reshape(1, d), w_gu, w_down]
    if final:
        in_specs.append(_const_spec((1, d)))
        args.append(g_final.reshape(1, d))
    return pl.pallas_call(
        functools.partial(_post_kernel, nb=nb, mix=mix, final=final),
        grid=(rows // tm,),
        in_specs=in_specs,
        out_specs=row(d),
        out_shape=jax.ShapeDtypeStruct((rows, d), F32),
        scratch_shapes=[pltpu.VMEM((tm, d), F32)],
        compiler_params=pltpu.CompilerParams(
            dimension_semantics=("arbitrary",), vmem_limit_bytes=VMEM_LIMIT),
        name="post_final" if final else "post",
    )(*args)


def _prep_w_in(w, mix, n_forget):
    wn = jnp.concatenate([w[:, :mix], w[:, 3 * mix + n_forget:]], axis=1)
    wt = w[:, mix:3 * mix + n_forget]
    if n_forget:
        wt = jnp.pad(wt, ((0, 0), (0, FORGET_ROWS - n_forget)))
    return wn.astype(BF16), wt.T.astype(BF16)


def _ff_chunk(d_ff):
    for c in (2816, 1408, 512, 256, 128):
        if d_ff % c == 0:
            return c
    raise ValueError(f"d_ff={d_ff} is not a multiple of {LANES}")


def _prep_ffn(w_gate_up, w_down):
    d, two_ff = w_gate_up.shape
    d_ff = two_ff // 2
    ffc = _ff_chunk(d_ff)
    n = d_ff // ffc
    gate = w_gate_up[:, :d_ff].reshape(d, n, ffc)
    up = w_gate_up[:, d_ff:].reshape(d, n, ffc)
    w_gu = jnp.transpose(jnp.concatenate([gate, up], axis=2), (1, 0, 2)).astype(BF16)
    return w_gu, w_down.reshape(n, ffc, d).astype(BF16)


def _feature_major(cache):
    *lead, t, h, hd = cache.shape
    return jnp.moveaxis(cache, -3, -1).reshape(*lead, h * hd, t)


def _token_major(xt):
    *lead, width, t = xt.shape
    return jnp.moveaxis(xt.reshape(*lead, width // HEAD_DIM, HEAD_DIM, t), -1, -3)


def kernel(x_prompt, x_sample, mem_prompt, cache_fox_k, cache_fox_v, cache_fox_logf, cache_sb_k, cache_sb_v, cache_mem_k, cache_mem_v, g_mix, w_in_fox, b_f, w_in_sb, g_mem, w_mem_kv, w_out, g_ffn, w_gate_up, w_down, g_final):
    bp, seq, d = x_prompt.shape
    bs, dec_seq, _ = x_sample.shape
    n_mem = mem_prompt.shape[1]
    depth = g_mix.shape[0]
    n_heads = b_f.shape[1]
    mix = n_heads * HEAD_DIM
    memw = w_mem_kv.shape[2] // 2
    past = cache_fox_k.shape[2]

    tk = 256
    ks = 512
    tq_p = ks
    tm_p = 512
    rows_s = bs * dec_seq
    assert seq % tq_p == 0 and seq % tm_p == 0 and rows_s % tk == 0
    assert dec_seq <= tk and past % ks == 0

    xp = x_prompt.reshape(bp * seq, d)
    xs = x_sample.reshape(rows_s, d)

    mkt_p, mvt_p = _memory_kv(mem_prompt, g_mem, jnp.transpose(w_mem_kv, (0, 2, 1)).astype(BF16))

    def new_keys(xt):
        xt = jnp.transpose(xt.reshape(mix, bs, dec_seq), (1, 0, 2))
        return jnp.pad(xt, ((0, 0), (0, 0), (0, tk - dec_seq))).astype(BF16)

    caches = {"fox": (_feature_major(cache_fox_k), _feature_major(cache_fox_v)),
              "sb": (_feature_major(cache_sb_k), _feature_major(cache_sb_v))}
    cmkt, cmvt = _feature_major(cache_mem_k), _feature_major(cache_mem_v)
    n_layers = {"fox": (depth + 1) // 2, "sb": depth // 2}
    kv_p = {"fox": None, "sb": None}
    fl_p = []
    fk_s, fv_s, fl_s, sk_s, sv_s = [], [], [], [], []
    for i in range(depth):
        j = i // 2
        is_fox = i % 2 == 0
        mode = "fox" if is_fox else "sb"
        if is_fox:
            wn, wt = _prep_w_in(w_in_fox[j], mix, n_heads)
            bias = jnp.pad(b_f[j], (0, FORGET_ROWS - n_heads)).reshape(FORGET_ROWS, 1)
        else:
            wn, wt = _prep_w_in(w_in_sb[j], mix, 0)
            bias = None
        w_gu, w_dn = _prep_ffn(w_gate_up[i], w_down[i])
        w_o = w_out[i].astype(BF16)
        g_fin = g_final if i == depth - 1 else None

        outs = _project(xp, g_mix[i], wn, wt, bias, batch=bp, mix=mix, memw=memw, tm=tm_p, tk=tk,
                        slot=j, n_slots=n_layers[mode], stacked=kv_p[mode])
        q, qm, kt, vt, ktb, vtb = outs[:6]
        kv_p[mode] = (kt, vt)
        c = None
        if is_fox:
            lft = outs[6]
            c = _key_bias_layout(_cumulative_logf(lft), n_heads, tk)
            fl_p.append(jnp.transpose(lft[:, :n_heads], (0, 2, 1)))
        o = _attention(q.reshape(bp, seq, mix), (ktb, vtb), c, mode=mode, tq=tq_p, tk=tk, ks=ks)
        xp_new = _post(xp, o.reshape(bp * seq, mix), qm, mkt_p, mvt_p, i,
                       w_o, g_ffn[i], w_gu, w_dn, g_fin, tm=tm_p, nb=1)

        outs = _project(xs, g_mix[i], wn, wt, bias, batch=1, mix=mix, memw=memw, tm=rows_s, tk=tk)
        q, qm, kt, vt = outs[:4]
        c = None
        if is_fox:
            lft = outs[6]
            lft_s = jnp.transpose(lft.reshape(FORGET_ROWS, bs, dec_seq), (1, 0, 2))
            lft_past = jnp.pad(jnp.transpose(cache_fox_logf[j], (0, 2, 1)),
                               ((0, 0), (0, FORGET_ROWS - n_heads), (0, 0)))
            lft_all = jnp.concatenate(
                [lft_past, jnp.pad(lft_s, ((0, 0), (0, 0), (0, tk - dec_seq)))], axis=2)
            c = _key_bias_layout(_cumulative_logf(lft_all), n_heads, tk)
            fl_s.append(jnp.transpose(lft_s[:, :n_heads], (0, 2, 1)))
        new_shape = (n_heads, HEAD_DIM, bs, dec_seq)
        k_new = jnp.transpose(kt.reshape(new_shape), (2, 3, 0, 1))
        v_new = jnp.transpose(vt.reshape(new_shape), (2, 3, 0, 1))
        if is_fox:
            fk_s.append(k_new)
            fv_s.append(v_new)
        else:
            sk_s.append(k_new)
            sv_s.append(v_new)
        o = _attention(q.reshape(bs, dec_seq, mix),
                       (*caches[mode], new_keys(kt), new_keys(vt)), c,
                       mode=mode, tq=dec_seq, tk=tk, ks=ks, past=past, cache_slot=j)
        xs_new = _post(xs, o.reshape(rows_s, mix), qm, cmkt, cmvt, i,
                       w_o, g_ffn[i], w_gu, w_dn, g_fin, tm=rows_s, nb=bs)
        xp, xs = xp_new, xs_new

    return (xp.reshape(bp, seq, d), xs.reshape(bs, dec_seq, d),
            _token_major(kv_p["fox"][0]), _token_major(kv_p["fox"][1]), jnp.stack(fl_p),
            _token_major(kv_p["sb"][0]), _token_major(kv_p["sb"][1]),
            _token_major(mkt_p), _token_major(mvt_p),
            jnp.stack(fk_s), jnp.stack(fv_s), jnp.stack(fl_s), jnp.stack(sk_s), jnp.stack(sv_s))
```
